```python
import math
import jax, jax.numpy as jnp
from jax import lax
import numpy as np

D_MODEL = 2048
BATCH = 4
SEQ = 2048
DEPTH = 2
DEC_BATCH = 128
DEC_SEQ = 4
PAST_LEN = 8192
PAGE_SIZE = 128

BRANCH_W = D_MODEL // 2
DIFF_HD = 128
DIFF_HEADS = BRANCH_W // (2 * DIFF_HD)
DIFF_KV_HEADS = 1
DIFF_GROUP = DIFF_HEADS // DIFF_KV_HEADS
FOX_HD = 128
FOX_HEADS = BRANCH_W // FOX_HD
FOX_KV_HEADS = 2
FOX_GROUP = FOX_HEADS // FOX_KV_HEADS
FOX_BIAS_LO = 2.0
FOX_BIAS_HI = 7.0
MLA_V = 128
MLA_HEADS = BRANCH_W // MLA_V
MLA_NOPE = 128
MLA_ROPE = 64
MLA_Q_LORA = D_MODEL // 4
MLA_KV_LORA = D_MODEL // 8
N_BRANCH = 3
D_FF = 4 * D_MODEL
ROPE_THETA = 10000.0
Q_BLOCK = 128
EPS = 1e-6
IN_SIZES = (
    DIFF_HEADS * 2 * DIFF_HD,
    DIFF_KV_HEADS * 2 * DIFF_HD,
    DIFF_KV_HEADS * 2 * DIFF_HD,
    FOX_HEADS * FOX_HD,
    FOX_KV_HEADS * FOX_HD,
    FOX_KV_HEADS * FOX_HD,
    FOX_HEADS,
    MLA_Q_LORA,
    MLA_KV_LORA + MLA_ROPE,
    N_BRANCH * D_MODEL,
)
IN_COLS = sum(IN_SIZES)
SPLIT_IDX = tuple(int(i) for i in np.cumsum(IN_SIZES)[:-1])

kernel_name = 'hybrid_diff_fox_mla_decoder_step'


def rms_norm(x, w):
    xf = x.astype(jnp.float32)
    y = xf * lax.rsqrt(jnp.mean(xf * xf, axis=-1, keepdims=True) + EPS)
    return (y * w.astype(jnp.float32)).astype(x.dtype)


def rope(x, pos):
    half = x.shape[-1] // 2
    inv_freq = 1.0 / (ROPE_THETA ** (jnp.arange(half, dtype=jnp.float32) / half))
    ang = pos.astype(jnp.float32)[:, None] * inv_freq[None, :]
    ang = ang.reshape((1, ang.shape[0]) + (1,) * (x.ndim - 3) + (half,))
    cos, sin = jnp.cos(ang), jnp.sin(ang)
    xf = x.astype(jnp.float32)
    x1, x2 = xf[..., :half], xf[..., half:]
    return jnp.concatenate([x1 * cos - x2 * sin, x2 * cos + x1 * sin], axis=-1).astype(x.dtype)


def sweep_query_blocks(fn, *q_arrays):
    T = q_arrays[0].shape[1]
    blk = Q_BLOCK if T % Q_BLOCK == 0 else T
    n = T // blk
    split = lambda a: jnp.moveaxis(a.reshape((a.shape[0], n, blk) + a.shape[2:]), 1, 0)
    out = lax.map(lambda blocks: fn(*blocks), tuple(split(a) for a in q_arrays))
    out = jnp.moveaxis(out, 0, 1)
    return out.reshape((out.shape[0], T) + out.shape[3:])


def causal_mask(qpos_blk, Tk):
    kpos = jnp.arange(Tk, dtype=jnp.int32)
    return kpos[None, None, :] <= qpos_blk[:, :, None]


def differential_attention(q, kv, qpos, lam, lam_init, subln_w):
    B, Tq = q.shape[:2]
    Tk = kv.shape[1]
    k = kv[:, :, 0].reshape(B, Tk, DIFF_KV_HEADS, 2, DIFF_HD)
    v = kv[:, :, 1]
    scale = DIFF_HD ** -0.5

    def block(qb, pb):
        s = jnp.einsum('bqkgmd,bskmd->bkgmqs', qb, k).astype(jnp.float32) * scale
        mask = causal_mask(pb, Tk)[:, None, None, None]
        p = jax.nn.softmax(jnp.where(mask, s, -jnp.inf), axis=-1)
        a = p[:, :, :, 0] - lam * p[:, :, :, 1]
        return jnp.einsum('bkgqs,bskd->bqkgd', a.astype(v.dtype), v)

    o = sweep_query_blocks(block, q, qpos)
    o = rms_norm(o.reshape(B, Tq, DIFF_HEADS, 2 * DIFF_HD), subln_w) * (1.0 - lam_init)
    return o.reshape(B, Tq, BRANCH_W)


def forgetting_attention(q, kv, logf, qpos):
    B, Tq = q.shape[:2]
    Tk = kv.shape[1]
    k, v = kv[:, :, 0], kv[:, :, 1]
    cum = jnp.cumsum(logf.astype(jnp.float32), axis=1).reshape(B, Tk, FOX_KV_HEADS, FOX_GROUP)
    cum_q = cum[:, Tk - Tq:]
    cum_k = jnp.moveaxis(cum, 1, 3)
    scale = FOX_HD ** -0.5

    def block(qb, pb, cb):
        s = jnp.einsum('bqkgd,bskd->bkgqs', qb, k).astype(jnp.float32) * scale
        s = s + jnp.moveaxis(cb, 1, 3)[..., None] - cum_k[:, :, :, None, :]
        mask = causal_mask(pb, Tk)[:, None, None]
        p = jax.nn.softmax(jnp.where(mask, s, -jnp.inf), axis=-1)
        return jnp.einsum('bkgqs,bskd->bqkgd', p.astype(v.dtype), v)

    o = sweep_query_blocks(block, q, qpos, cum_q)
    return o.reshape(B, Tq, BRANCH_W)


def latent_attention(q_nope, q_pe, kv, qpos, w_kvb):
    B, Tq = q_nope.shape[:2]
    Tk = kv.shape[1]
    ckv, kpe = kv[..., :MLA_KV_LORA], kv[..., MLA_KV_LORA:]
    w_uk, w_uv = w_kvb[..., :MLA_NOPE], w_kvb[..., MLA_NOPE:]
    q_lat = jnp.einsum('bqhd,chd->bqhc', q_nope, w_uk)
    scale = (MLA_NOPE + MLA_ROPE) ** -0.5

    def block(qlb, qpb, pb):
        s = (jnp.einsum('bqhc,bsc->bhqs', qlb, ckv) + jnp.einsum('bqhr,bsr->bhqs', qpb, kpe)).astype(jnp.float32) * scale
        mask = causal_mask(pb, Tk)[:, None]
        p = jax.nn.softmax(jnp.where(mask, s, -jnp.inf), axis=-1)
        return jnp.einsum('bhqs,bsc->bqhc', p.astype(ckv.dtype), ckv)

    o_lat = sweep_query_blocks(block, q_lat, q_pe, qpos)
    o = jnp.einsum('bqhc,chd->bqhd', o_lat, w_uv)
    return o.reshape(B, Tq, BRANCH_W)


def gather_pages(cache_l, page_table):
    g = jnp.take(cache_l, page_table, axis=0)
    return g.reshape((g.shape[0], g.shape[1] * g.shape[2]) + g.shape[3:])


def decoder_layer(x, past, layer_idx, g_mix_pre, w_in, b_fox_f, b_gate, diff_lambda, diff_subln,
                  g_mla_qa, w_mla_qb, g_mla_kva, w_mla_kvb, w_branch, w_o, g_mix_post,
                  g_mlp_pre, w_up, w_down, g_mlp_post):
    B, T, _ = x.shape
    past_len = 0 if past is None else past[0].shape[1]
    pos = past_len + jnp.arange(T, dtype=jnp.int32)
    qpos = jnp.broadcast_to(pos, (B, T))

    def extend(i, new):
        return new if past is None else jnp.concatenate([past[i], new], axis=1)

    u = rms_norm(x, g_mix_pre)
    proj = jnp.einsum('btd,dc->btc', u, w_in)
    dq, dk, dv, fq, fk, fv, ff, mqa, mkva, gate = jnp.split(proj, SPLIT_IDX, axis=-1)

    dq = rope(dq.reshape(B, T, DIFF_KV_HEADS, DIFF_GROUP, 2, DIFF_HD), pos)
    dk = rope(dk.reshape(B, T, DIFF_KV_HEADS, 2, DIFF_HD), pos)
    new_diff = jnp.stack([dk.reshape(B, T, DIFF_KV_HEADS, 2 * DIFF_HD),
                          dv.reshape(B, T, DIFF_KV_HEADS, 2 * DIFF_HD)], axis=2)
    lam_init = 0.8 - 0.6 * math.exp(-0.3 * layer_idx)
    lf = diff_lambda.astype(jnp.float32)
    lam = jnp.exp(jnp.sum(lf[0] * lf[1])) - jnp.exp(jnp.sum(lf[2] * lf[3])) + lam_init
    o_diff = differential_attention(dq, extend(0, new_diff), qpos, lam, lam_init, diff_subln)

    fq = fq.reshape(B, T, FOX_KV_HEADS, FOX_GROUP, FOX_HD)
    new_fox = jnp.stack([fk.reshape(B, T, FOX_KV_HEADS, FOX_HD),
                         fv.reshape(B, T, FOX_KV_HEADS, FOX_HD)], axis=2)
    new_logf = jax.nn.log_sigmoid(ff.astype(jnp.float32) + b_fox_f.astype(jnp.float32))
    o_fox = forgetting_attention(fq, extend(1, new_fox), extend(2, new_logf), qpos)

    c_q = rms_norm(mqa, g_mla_qa)
    q = jnp.einsum('btc,chd->bthd', c_q, w_mla_qb)
    q_nope, q_pe = q[..., :MLA_NOPE], rope(q[..., MLA_NOPE:], pos)
    c_kv = rms_norm(mkva[..., :MLA_KV_LORA], g_mla_kva)
    k_pe = rope(mkva[..., MLA_KV_LORA:], pos)
    new_mla = jnp.concatenate([c_kv, k_pe], axis=-1)
    o_mla = latent_attention(q_nope, q_pe, extend(3, new_mla), qpos, w_mla_kvb)

    o = jnp.stack([o_diff, o_fox, o_mla], axis=2)
    y = jnp.einsum('btnc,ncd->btnd', o, w_branch)
    g = jax.nn.sigmoid(gate.reshape(B, T, N_BRANCH, D_MODEL) + b_gate.reshape(N_BRANCH, D_MODEL))
    mix = jnp.einsum('btd,de->bte', jnp.sum(g * y, axis=2), w_o)
    x = x + rms_norm(mix, g_mix_post)

    h = jnp.square(jax.nn.relu(jnp.einsum('btd,df->btf', rms_norm(x, g_mlp_pre), w_up)))
    x = x + rms_norm(jnp.einsum('btf,fd->btd', h, w_down), g_mlp_post)
    return x, (new_diff, new_fox, new_logf, new_mla)


def setup_inputs(seed: int = 0) -> dict:
    key = jax.random.key(seed)
    ks = jax.random.split(key, 26)
    n_pages = PAST_LEN // PAGE_SIZE
    n_used = DEC_BATCH * n_pages
    n_phys = n_used + max(1, n_used // 4)
    nrm = lambda k, shape, scale=1.0: jax.random.normal(k, shape, jnp.float32) * scale
    gain = lambda k, shape: 1.0 + 0.1 * jax.random.normal(k, shape, jnp.float32)
    fox_bias = jnp.linspace(FOX_BIAS_LO, FOX_BIAS_HI, FOX_HEADS, dtype=jnp.float32)
    return {
        'x_prompt': nrm(ks[0], (BATCH, SEQ, D_MODEL)),
        'x_sample': nrm(ks[1], (DEC_BATCH, DEC_SEQ, D_MODEL)),
        'cache_diff_kv': nrm(ks[2], (DEPTH, n_phys, PAGE_SIZE, 2, DIFF_KV_HEADS, 2 * DIFF_HD)),
        'cache_fox_kv': nrm(ks[3], (DEPTH, n_phys, PAGE_SIZE, 2, FOX_KV_HEADS, FOX_HD)),
        'cache_fox_logf': jax.nn.log_sigmoid(fox_bias + 0.5 * nrm(ks[4], (DEPTH, n_phys, PAGE_SIZE, FOX_HEADS))),
        'cache_mla_kv': nrm(ks[5], (DEPTH, n_phys, PAGE_SIZE, MLA_KV_LORA + MLA_ROPE)),
        'page_table': jax.random.permutation(ks[6], n_phys)[:n_used].reshape(DEC_BATCH, n_pages).astype(jnp.int32),
        'g_mix_pre': gain(ks[7], (DEPTH, D_MODEL)),
        'w_in': nrm(ks[8], (DEPTH, D_MODEL, IN_COLS), D_MODEL ** -0.5),
        'b_fox_f': fox_bias + 0.1 * nrm(ks[9], (DEPTH, FOX_HEADS)),
        'b_gate': nrm(ks[10], (DEPTH, N_BRANCH * D_MODEL), 0.02),
        'diff_lambda': nrm(ks[11], (DEPTH, 4, DIFF_HD), 0.1),
        'diff_subln': gain(ks[12], (DEPTH, 2 * DIFF_HD)),
        'g_mla_qa': gain(ks[13], (DEPTH, MLA_Q_LORA)),
        'w_mla_qb': nrm(ks[14], (DEPTH, MLA_Q_LORA, MLA_HEADS, MLA_NOPE + MLA_ROPE), MLA_Q_LORA ** -0.5),
        'g_mla_kva': gain(ks[15], (DEPTH, MLA_KV_LORA)),
        'w_mla_kvb': nrm(ks[16], (DEPTH, MLA_KV_LORA, MLA_HEADS, MLA_NOPE + MLA_V), MLA_KV_LORA ** -0.5),
        'w_branch': nrm(ks[17], (DEPTH, N_BRANCH, BRANCH_W, D_MODEL), BRANCH_W ** -0.5),
        'w_o': nrm(ks[18], (DEPTH, D_MODEL, D_MODEL), D_MODEL ** -0.5),
        'g_mix_post': gain(ks[19], (DEPTH, D_MODEL)),
        'g_mlp_pre': gain(ks[20], (DEPTH, D_MODEL)),
        'w_up': nrm(ks[21], (DEPTH, D_MODEL, D_FF), D_MODEL ** -0.5),
        'w_down': nrm(ks[22], (DEPTH, D_FF, D_MODEL), D_FF ** -0.5),
        'g_mlp_post': gain(ks[23], (DEPTH, D_MODEL)),
    }


def reference(x_prompt, x_sample, cache_diff_kv, cache_fox_kv, cache_fox_logf, cache_mla_kv, page_table,
              g_mix_pre, w_in, b_fox_f, b_gate, diff_lambda, diff_subln, g_mla_qa, w_mla_qb, g_mla_kva,
              w_mla_kvb, w_branch, w_o, g_mix_post, g_mlp_pre, w_up, w_down, g_mlp_post):
    y_p, y_s = x_prompt, x_sample
    st_p, st_s = [], []
    for l in range(DEPTH):
        lw = (g_mix_pre[l], w_in[l], b_fox_f[l], b_gate[l], diff_lambda[l], diff_subln[l],
              g_mla_qa[l], w_mla_qb[l], g_mla_kva[l], w_mla_kvb[l], w_branch[l], w_o[l],
              g_mix_post[l], g_mlp_pre[l], w_up[l], w_down[l], g_mlp_post[l])
        y_p, new_p = decoder_layer(y_p, None, l, *lw)
        past = (gather_pages(cache_diff_kv[l], page_table), gather_pages(cache_fox_kv[l], page_table),
                gather_pages(cache_fox_logf[l], page_table), gather_pages(cache_mla_kv[l], page_table))
        y_s, new_s = decoder_layer(y_s, past, l, *lw)
        st_p.append(new_p)
        st_s.append(new_s)
    stack = lambda sts, i: jnp.stack([s[i] for s in sts], axis=0)
    diff_kv_p, fox_kv_p, fox_logf_p, mla_kv_p = stack(st_p, 0), stack(st_p, 1), stack(st_p, 2), stack(st_p, 3)
    diff_kv_s, fox_kv_s, fox_logf_s, mla_kv_s = stack(st_s, 0), stack(st_s, 1), stack(st_s, 2), stack(st_s, 3)
    return (y_p, y_s, diff_kv_p, fox_kv_p, fox_logf_p, mla_kv_p, diff_kv_s, fox_kv_s, fox_logf_s, mla_kv_s)
```

```python
import functools
import math

import numpy as np
import jax
import jax.numpy as jnp
from jax import lax
from jax.experimental import pallas as pl
from jax.experimental.pallas import tpu as pltpu

D_MODEL = 2048
BATCH = 4
SEQ = 2048
DEPTH = 2
DEC_BATCH = 128
DEC_SEQ = 4
PAST_LEN = 8192
PAGE_SIZE = 128
N_PAGES = PAST_LEN // PAGE_SIZE
BRANCH_W = D_MODEL // 2
DIFF_HD = 128
DIFF_HEADS = BRANCH_W // (2 * DIFF_HD)
FOX_HD = 128
FOX_HEADS = BRANCH_W // FOX_HD
FOX_KV_HEADS = 2
FOX_GROUP = FOX_HEADS // FOX_KV_HEADS
MLA_V = 128
MLA_HEADS = BRANCH_W // MLA_V
MLA_NOPE = 128
MLA_ROPE = 64
MLA_Q_LORA = D_MODEL // 4
MLA_KV_LORA = D_MODEL // 8
N_BRANCH = 3
D_FF = 4 * D_MODEL
ROPE_THETA = 10000.0
EPS = 1e-6

M_PROMPT = BATCH * SEQ
M_SAMPLE = DEC_BATCH * DEC_SEQ
M_ALL = M_PROMPT + M_SAMPLE

LANES = 128
NEG = -1e30
VMEM_LIMIT = 48 * 1024 * 1024

C_DQ = 0
C_FQ = 1024
C_DKV = 2048
C_FKV = 2560
C_MQA = 3072
C_MKV = 3584
C_GATE = 4096
IN_COLS_PAD = C_GATE + N_BRANCH * D_MODEL
MLA_QW = MLA_KV_LORA + LANES
MLA_KW = MLA_KV_LORA + LANES

BF16 = jnp.bfloat16
F32 = jnp.float32


def _cparams(sem):
    return pltpu.CompilerParams(dimension_semantics=sem, vmem_limit_bytes=VMEM_LIMIT)


def _dot(a, b):
    return jnp.dot(a, b, preferred_element_type=F32)


def _dot_nt(a, b):
    return lax.dot_general(a, b, (((1,), (1,)), ((), ())), preferred_element_type=F32)


def _dot_exact(a, b):
    return jnp.dot(a, b, preferred_element_type=F32, precision=lax.Precision.HIGHEST)


def _rms(x, g):
    ms = jnp.mean(x * x, axis=-1, keepdims=True)
    return x * lax.rsqrt(ms + EPS) * g


def _norm_matmul_kernel(x_ref, g_ref, w_ref, o_ref, u_ref, *, act):
    @pl.when(pl.program_id(1) == 0)
    def _():
        u_ref[...] = _rms(x_ref[...].astype(F32), g_ref[...]).astype(BF16)

    acc = _dot(u_ref[...], w_ref[...])
    if act == "relu2":
        acc = jnp.square(jnp.maximum(acc, 0.0))
    o_ref[...] = acc.astype(o_ref.dtype)


def norm_matmul(x, x_col, k, g, w, *, tm, tn, act=None, out_dtype=F32, name):
    m = x.shape[0]
    n = w.shape[1]
    return pl.pallas_call(
        functools.partial(_norm_matmul_kernel, act=act),
        out_shape=jax.ShapeDtypeStruct((m, n), out_dtype),
        grid=(m // tm, n // tn),
        in_specs=[
            pl.BlockSpec((tm, k), lambda i, j: (i, x_col)),
            pl.BlockSpec((1, k), lambda i, j: (0, 0)),
            pl.BlockSpec((k, tn), lambda i, j: (0, j)),
        ],
        out_specs=pl.BlockSpec((tm, tn), lambda i, j: (i, j)),
        scratch_shapes=[pltpu.VMEM((tm, k), BF16)],
        compiler_params=_cparams(("parallel", "arbitrary")),
        name=name,
    )(x, g.reshape(1, k), w)


def _matmul_norm_res_kernel(a_ref, w_ref, g_ref, r_ref, o_ref, acc_ref, *, nk):
    k = pl.program_id(1)

    @pl.when(k == 0)
    def _():
        acc_ref[...] = jnp.zeros_like(acc_ref)

    acc_ref[...] += _dot(a_ref[...], w_ref[...])

    @pl.when(k == nk - 1)
    def _():
        o_ref[...] = r_ref[...] + _rms(acc_ref[...], g_ref[...])


def matmul_norm_res(a, w, g, resid, *, tm, tk, name):
    m, kdim = a.shape
    n = w.shape[1]
    nk = kdim // tk
    return pl.pallas_call(
        functools.partial(_matmul_norm_res_kernel, nk=nk),
        out_shape=jax.ShapeDtypeStruct((m, n), F32),
        grid=(m // tm, nk),
        in_specs=[
            pl.BlockSpec((tm, tk), lambda i, k: (i, k)),
            pl.BlockSpec((tk, n), lambda i, k: (k, 0)),
            pl.BlockSpec((1, n), lambda i, k: (0, 0)),
            pl.BlockSpec((tm, n), lambda i, k: (i, 0)),
        ],
        out_specs=pl.BlockSpec((tm, n), lambda i, k: (i, 0)),
        scratch_shapes=[pltpu.VMEM((tm, n), F32)],
        compiler_params=_cparams(("parallel", "arbitrary")),
        name=name,
    )(a, w, g.reshape(1, n), resid)


def _merge_kernel(o_ref, g0_ref, g1_ref, g2_ref, b_ref, wb_ref, m_ref):
    acc = None
    for n, g_ref in enumerate((g0_ref, g1_ref, g2_ref)):
        y = _dot(o_ref[:, n * BRANCH_W:(n + 1) * BRANCH_W], wb_ref[n])
        term = jax.nn.sigmoid(g_ref[...] + b_ref[n]) * y
        acc = term if acc is None else acc + term
    m_ref[...] = acc.astype(m_ref.dtype)


def gated_merge(o_cat, proj, b_gate, w_branch, *, tm, tn):
    m = o_cat.shape[0]
    nj = D_MODEL // tn
    gate_spec = lambda n: pl.BlockSpec(
        (tm, tn), lambda i, j, n=n: (i, (C_GATE + n * D_MODEL) // tn + j))
    return pl.pallas_call(
        _merge_kernel,
        out_shape=jax.ShapeDtypeStruct((m, D_MODEL), BF16),
        grid=(m // tm, nj),
        in_specs=[
            pl.BlockSpec((tm, N_BRANCH * BRANCH_W), lambda i, j: (i, 0)),
            gate_spec(0), gate_spec(1), gate_spec(2),
            pl.BlockSpec((N_BRANCH, 1, tn), lambda i, j: (0, 0, j)),
            pl.BlockSpec((N_BRANCH, BRANCH_W, tn), lambda i, j: (0, 0, j)),
        ],
        out_specs=pl.BlockSpec((tm, tn), lambda i, j: (i, j)),
        compiler_params=_cparams(("parallel", "arbitrary")),
        name="gated_merge",
    )(o_cat, proj, proj, proj, b_gate.reshape(N_BRANCH, 1, D_MODEL), w_branch)


def _rope128(x, cos_f, sin_s):
    return x * cos_f + pltpu.roll(x, DIFF_HD // 2, 1) * sin_s


def _rope64(x, cos_f, sin_s):
    lane = lax.broadcasted_iota(jnp.int32, x.shape, 1)
    half = MLA_ROPE // 2
    rot = jnp.where((lane % MLA_ROPE) < half,
                    pltpu.roll(x, LANES - half, 1), pltpu.roll(x, half, 1))
    return x * cos_f + rot * sin_s


def _log_sigmoid(x):
    return jnp.minimum(x, 0.0) - jnp.log1p(jnp.exp(-jnp.abs(x)))


def _inproj_epilogue_kernel(dq_ref, fq_ref, dkv_ref, fkv_ref, mkv_ref, c128_ref, s128_ref,
                            c64_ref, s64_ref, gkv_ref, bf_ref,
                            dq_o, fq_o, dkv_o, dkv_bf_o, fkv_bf_o, logf_o, logfp_o, mla_o, mla_bf_o):
    cos_f, sin_s = c128_ref[...], s128_ref[...]
    qscale = DIFF_HD ** -0.5
    for c in range(2 * DIFF_HEADS):
        sl = slice(c * DIFF_HD, (c + 1) * DIFF_HD)
        dq_o[:, sl] = (_rope128(dq_ref[:, sl], cos_f, sin_s) * qscale).astype(BF16)
    fq_o[...] = (fq_ref[...] * (FOX_HD ** -0.5)).astype(BF16)
    for c in range(2):
        sl = slice(c * DIFF_HD, (c + 1) * DIFF_HD)
        kr = _rope128(dkv_ref[:, sl], cos_f, sin_s)
        dkv_o[:, sl] = kr
        dkv_bf_o[:, sl] = kr.astype(BF16)
    v = dkv_ref[:, 2 * DIFF_HD:]
    dkv_o[:, 2 * DIFF_HD:] = v
    dkv_bf_o[:, 2 * DIFF_HD:] = v.astype(BF16)
    fkv_bf_o[...] = fkv_ref[...].astype(BF16)
    lf = _log_sigmoid(mkv_ref[:, MLA_KV_LORA + LANES:] + bf_ref[...])
    logfp_o[...] = lf
    logf_o[...] = lf[:, :FOX_HEADS]
    ckv = _rms(mkv_ref[:, :MLA_KV_LORA], gkv_ref[...])
    kpe = _rope64(mkv_ref[:, MLA_KV_LORA:MLA_KV_LORA + LANES], c64_ref[...], s64_ref[...])
    mla_o[:, :MLA_KV_LORA] = ckv
    mla_o[:, MLA_KV_LORA:] = kpe[:, :MLA_ROPE]
    mla_bf_o[:, :MLA_KV_LORA] = ckv.astype(BF16)
    mla_bf_o[:, MLA_KV_LORA:] = kpe.astype(BF16)


def inproj_epilogue(proj, tabs, g_kva, b_fox_pad, *, tm):
    m = proj.shape[0]
    c128, s128, c64, s64 = tabs
    row = lambda w, c: pl.BlockSpec((tm, w), lambda i, c=c: (i, c))
    full = lambda w: pl.BlockSpec((1, w), lambda i: (0, 0))
    outs = [
        ((m, 1024), BF16), ((m, 1024), BF16), ((m, 512), F32), ((m, 512), BF16), ((m, 512), BF16),
        ((m, FOX_HEADS), F32), ((m, LANES), F32), ((m, MLA_KV_LORA + MLA_ROPE), F32), ((m, MLA_KW), BF16),
    ]
    return pl.pallas_call(
        _inproj_epilogue_kernel,
        out_shape=[jax.ShapeDtypeStruct(s, d) for s, d in outs],
        grid=(m // tm,),
        in_specs=[
            row(1024, C_DQ // 1024), row(1024, C_FQ // 1024), row(512, C_DKV // 512),
            row(512, C_FKV // 512), row(512, C_MKV // 512),
            row(LANES, 0), row(LANES, 0), row(LANES, 0), row(LANES, 0),
            full(MLA_KV_LORA), full(LANES),
        ],
        out_specs=[pl.BlockSpec((tm, s[1]), lambda i: (i, 0)) for s, _ in outs],
        compiler_params=_cparams(("parallel",)),
        name="inproj_epilogue",
    )(proj, proj, proj, proj, proj, c128, s128, c64, s64, g_kva.reshape(1, -1), b_fox_pad)


def _mla_q_kernel(q_ref, wuk_ref, c64_ref, s64_ref, o_ref):
    scale = (MLA_NOPE + MLA_ROPE) ** -0.5
    for h in range(MLA_HEADS):
        qn = q_ref[:, h * MLA_NOPE:(h + 1) * MLA_NOPE].astype(BF16)
        qlat = _dot(qn, wuk_ref[h])
        base = MLA_HEADS * MLA_NOPE + h * LANES
        qpe = _rope64(q_ref[:, base:base + LANES], c64_ref[...], s64_ref[...])
        o_ref[:, h * MLA_QW:h * MLA_QW + MLA_KV_LORA] = (qlat * scale).astype(BF16)
        o_ref[:, h * MLA_QW + MLA_KV_LORA:(h + 1) * MLA_QW] = (qpe * scale).astype(BF16)


def mla_q_prep(q, w_uk, c64, s64, *, tm):
    m, qw = q.shape
    return pl.pallas_call(
        _mla_q_kernel,
        out_shape=jax.ShapeDtypeStruct((m, MLA_HEADS * MLA_QW), BF16),
        grid=(m // tm,),
        in_specs=[
            pl.BlockSpec((tm, qw), lambda i: (i, 0)),
            pl.BlockSpec((MLA_HEADS, MLA_NOPE, MLA_KV_LORA), lambda i: (0, 0, 0)),
            pl.BlockSpec((tm, LANES), lambda i: (i, 0)),
            pl.BlockSpec((tm, LANES), lambda i: (i, 0)),
        ],
        out_specs=pl.BlockSpec((tm, MLA_HEADS * MLA_QW), lambda i: (i, 0)),
        compiler_params=_cparams(("parallel",)),
        name="mla_q_prep",
    )(q, w_uk, c64, s64)


def _mla_out_kernel(o_ref, wuv_ref, y_ref):
    for h in range(MLA_HEADS):
        y = _dot(o_ref[:, h * MLA_KV_LORA:(h + 1) * MLA_KV_LORA], wuv_ref[h])
        y_ref[:, h * MLA_V:(h + 1) * MLA_V] = y.astype(BF16)


def mla_out_proj(o_lat, w_uv, *, tm):
    m = o_lat.shape[0]
    return pl.pallas_call(
        _mla_out_kernel,
        out_shape=jax.ShapeDtypeStruct((m, BRANCH_W), BF16),
        grid=(m // tm,),
        in_specs=[
            pl.BlockSpec((tm, MLA_HEADS * MLA_KV_LORA), lambda i: (i, 0)),
            pl.BlockSpec((MLA_HEADS, MLA_KV_LORA, MLA_V), lambda i: (0, 0, 0)),
        ],
        out_specs=pl.BlockSpec((tm, BRANCH_W), lambda i: (i, 0)),
        compiler_params=_cparams(("parallel",)),
        name="mla_out_proj",
    )(o_lat, w_uv)


CUM_BLK = 256


def _fox_cum_prompt_kernel(x_ref, xt_ref, cum_ref, cumt_ref):
    r = lax.broadcasted_iota(jnp.int32, (CUM_BLK, CUM_BLK), 0)
    c = lax.broadcasted_iota(jnp.int32, (CUM_BLK, CUM_BLK), 1)
    lower = (c <= r).astype(F32)
    upper = (r <= c).astype(F32)
    carry = jnp.zeros((1, LANES), F32)
    carry_t = jnp.zeros((FOX_HEADS, 1), F32)
    for b in range(SEQ // CUM_BLK):
        sl = slice(b * CUM_BLK, (b + 1) * CUM_BLK)
        cb = _dot_exact(lower, x_ref[sl, :]) + carry
        cum_ref[sl, :] = cb
        carry = cb[CUM_BLK - 1:CUM_BLK, :]
        cbt = _dot_exact(xt_ref[0, :, sl], upper) + carry_t
        cumt_ref[0, :, sl] = cbt
        carry_t = cbt[:, CUM_BLK - 1:CUM_BLK]


def fox_cum_prompt(logf_pad, logf_t):
    return pl.pallas_call(
        _fox_cum_prompt_kernel,
        out_shape=[jax.ShapeDtypeStruct((M_PROMPT, LANES), F32),
                   jax.ShapeDtypeStruct((BATCH, FOX_HEADS, SEQ), F32)],
        grid=(BATCH,),
        in_specs=[pl.BlockSpec((SEQ, LANES), lambda b: (b, 0)),
                  pl.BlockSpec((1, FOX_HEADS, SEQ), lambda b: (b, 0, 0))],
        out_specs=[pl.BlockSpec((SEQ, LANES), lambda b: (b, 0)),
                   pl.BlockSpec((1, FOX_HEADS, SEQ), lambda b: (b, 0, 0))],
        compiler_params=_cparams(("parallel",)),
        name="fox_cum_prompt",
    )(logf_pad, logf_t)


def _softmax_init(m_ref, l_ref, acc_ref):
    m_ref[...] = jnp.full_like(m_ref, NEG)
    l_ref[...] = jnp.zeros_like(l_ref)
    acc_ref[...] = jnp.zeros_like(acc_ref)


def _softmax_step(s, pv_fn, m_ref, l_ref, acc_ref, idx):
    m_prev = m_ref[idx]
    m_new = jnp.maximum(m_prev, jnp.max(s, axis=-1, keepdims=True))
    alpha = jnp.exp(m_prev - m_new)
    p = jnp.exp(s - m_new)
    l_ref[idx] = alpha * l_ref[idx] + jnp.sum(p, axis=-1, keepdims=True)
    acc_ref[idx] = alpha * acc_ref[idx] + pv_fn(p.astype(BF16))
    m_ref[idx] = m_new


def _causal_mask(qi, ki, tq, tk):
    qpos = qi * tq + lax.broadcasted_iota(jnp.int32, (tq, tk), 0)
    kpos = ki * tk + lax.broadcasted_iota(jnp.int32, (tq, tk), 1)
    return kpos <= qpos


def _diff_lambda(lam_ref, lam_init):
    lf = lam_ref[...]
    a = jnp.sum(lf[0:1] * lf[1:2], axis=-1, keepdims=True)
    b = jnp.sum(lf[2:3] * lf[3:4], axis=-1, keepdims=True)
    return jnp.exp(a) - jnp.exp(b) + lam_init


def _diff_finish(o1, o2, lam, subln, lam_init):
    return _rms(o1 - lam * o2, subln) * (1.0 - lam_init)


def _flash_diff_kernel(q_ref, kv_ref, lam_ref, subln_ref, o_ref, m_ref, l_ref, acc_ref,
                       *, tq, tk, nk, lam_init):
    qi, ki = pl.program_id(1), pl.program_id(2)

    @pl.when(ki == 0)
    def _():
        _softmax_init(m_ref, l_ref, acc_ref)

    @pl.when(ki * tk <= qi * tq + tq - 1)
    def _():
        mask = _causal_mask(qi, ki, tq, tk)
        v = kv_ref[:, 2 * DIFF_HD:]
        for c in range(2 * DIFF_HEADS):
            mp = c % 2
            s = _dot_nt(q_ref[:, c * DIFF_HD:(c + 1) * DIFF_HD], kv_ref[:, mp * DIFF_HD:(mp + 1) * DIFF_HD])
            s = jnp.where(mask, s, NEG)
            _softmax_step(s, lambda p: _dot(p, v), m_ref, l_ref, acc_ref, c)

    @pl.when(ki == nk - 1)
    def _():
        lam = _diff_lambda(lam_ref, lam_init)
        for h in range(DIFF_HEADS):
            o1 = acc_ref[2 * h] / l_ref[2 * h]
            o2 = acc_ref[2 * h + 1] / l_ref[2 * h + 1]
            o = _diff_finish(o1, o2, lam, subln_ref[...], lam_init)
            o_ref[:, h * 2 * DIFF_HD:(h + 1) * 2 * DIFF_HD] = o.astype(BF16)


def _kv_clamp(qi, ki, tq, tk):
    return jnp.minimum(ki, (qi * tq + tq - 1) // tk)


def flash_diff(dq, dkv_bf, diff_lambda, subln, lam_init, *, tq, tk):
    nq, nk = SEQ // tq, SEQ // tk
    return pl.pallas_call(
        functools.partial(_flash_diff_kernel, tq=tq, tk=tk, nk=nk, lam_init=lam_init),
        out_shape=jax.ShapeDtypeStruct((M_PROMPT, BRANCH_W), BF16),
        grid=(BATCH, nq, nk),
        in_specs=[
            pl.BlockSpec((tq, BRANCH_W), lambda b, qi, ki: (b * nq + qi, 0)),
            pl.BlockSpec((tk, 512), lambda b, qi, ki: (b * nk + _kv_clamp(qi, ki, tq, tk), 0)),
            pl.BlockSpec((4, DIFF_HD), lambda b, qi, ki: (0, 0)),
            pl.BlockSpec((1, 2 * DIFF_HD), lambda b, qi, ki: (0, 0)),
        ],
        out_specs=pl.BlockSpec((tq, BRANCH_W), lambda b, qi, ki: (b * nq + qi, 0)),
        scratch_shapes=[pltpu.VMEM((2 * DIFF_HEADS, tq, 1), F32), pltpu.VMEM((2 * DIFF_HEADS, tq, 1), F32),
                        pltpu.VMEM((2 * DIFF_HEADS, tq, 2 * DIFF_HD), F32)],
        compiler_params=_cparams(("parallel", "parallel", "arbitrary")),
        name="flash_diff",
    )(dq, dkv_bf, diff_lambda, subln.reshape(1, -1))


def _flash_fox_kernel(q_ref, k_ref, v_ref, cq_ref, ck_ref, o_ref, m_ref, l_ref, acc_ref, *, tq, tk, nk):
    g, qi, ki = pl.program_id(1), pl.program_id(2), pl.program_id(3)

    @pl.when(ki == 0)
    def _():
        _softmax_init(m_ref, l_ref, acc_ref)

    @pl.when(ki * tk <= qi * tq + tq - 1)
    def _():
        mask = _causal_mask(qi, ki, tq, tk)
        k, v = k_ref[...], v_ref[...]
        lane = lax.broadcasted_iota(jnp.int32, (tq, LANES), 1)
        sub = lax.broadcasted_iota(jnp.int32, (FOX_HEADS, tk), 0)
        for hl in range(FOX_GROUP):
            h = g * FOX_GROUP + hl
            cq = jnp.sum(jnp.where(lane == h, cq_ref[...], 0.0), axis=-1, keepdims=True)
            ck = jnp.sum(jnp.where(sub == h, ck_ref[0], 0.0), axis=0, keepdims=True)
            s = _dot_nt(q_ref[:, hl * FOX_HD:(hl + 1) * FOX_HD], k) + cq - ck
            s = jnp.where(mask, s, NEG)
            _softmax_step(s, lambda p: _dot(p, v), m_ref, l_ref, acc_ref, hl)

    @pl.when(ki == nk - 1)
    def _():
        for hl in range(FOX_GROUP):
            o_ref[:, hl * FOX_HD:(hl + 1) * FOX_HD] = (acc_ref[hl] / l_ref[hl]).astype(BF16)


def flash_fox(fq, fkv_bf, cum_pad, cum_t, *, tq, tk):
    nq, nk = SEQ // tq, SEQ // tk
    gw = FOX_GROUP * FOX_HD
    return pl.pallas_call(
        functools.partial(_flash_fox_kernel, tq=tq, tk=tk, nk=nk),
        out_shape=jax.ShapeDtypeStruct((M_PROMPT, BRANCH_W), BF16),
        grid=(BATCH, FOX_KV_HEADS, nq, nk),
        in_specs=[
            pl.BlockSpec((tq, gw), lambda b, g, qi, ki: (b * nq + qi, g)),
            pl.BlockSpec((tk, FOX_HD), lambda b, g, qi, ki: (b * nk + _kv_clamp(qi, ki, tq, tk), g)),
            pl.BlockSpec((tk, FOX_HD), lambda b, g, qi, ki: (b * nk + _kv_clamp(qi, ki, tq, tk), FOX_KV_HEADS + g)),
            pl.BlockSpec((tq, LANES), lambda b, g, qi, ki: (b * nq + qi, 0)),
            pl.BlockSpec((1, FOX_HEADS, tk), lambda b, g, qi, ki: (b, 0, _kv_clamp(qi, ki, tq, tk))),
        ],
        out_specs=pl.BlockSpec((tq, gw), lambda b, g, qi, ki: (b * nq + qi, g)),
        scratch_shapes=[pltpu.VMEM((FOX_GROUP, tq, 1), F32), pltpu.VMEM((FOX_GROUP, tq, 1), F32),
                        pltpu.VMEM((FOX_GROUP, tq, FOX_HD), F32)],
        compiler_params=_cparams(("parallel", "parallel", "parallel", "arbitrary")),
        name="flash_fox",
    )(fq, fkv_bf, fkv_bf, cum_pad, cum_t)


def _flash_mla_kernel(q_ref, kv_ref, o_ref, m_ref, l_ref, acc_ref, *, tq, tk, nk):
    qi, ki = pl.program_id(1), pl.program_id(2)

    @pl.when(ki == 0)
    def _():
        _softmax_init(m_ref, l_ref, acc_ref)

    @pl.when(ki * tk <= qi * tq + tq - 1)
    def _():
        mask = _causal_mask(qi, ki, tq, tk)
        kv = kv_ref[...]
        v = kv_ref[:, :MLA_KV_LORA]
        for h in range(MLA_HEADS):
            s = _dot_nt(q_ref[:, h * MLA_QW:(h + 1) * MLA_QW], kv)
            s = jnp.where(mask, s, NEG)
            _softmax_step(s, lambda p: _dot(p, v), m_ref, l_ref, acc_ref, h)

    @pl.when(ki == nk - 1)
    def _():
        for h in range(MLA_HEADS):
            o_ref[:, h * MLA_KV_LORA:(h + 1) * MLA_KV_LORA] = (acc_ref[h] / l_ref[h]).astype(BF16)


def flash_mla(qcat, mla_bf, *, tq, tk):
    nq, nk = SEQ // tq, SEQ // tk
    return pl.pallas_call(
        functools.partial(_flash_mla_kernel, tq=tq, tk=tk, nk=nk),
        out_shape=jax.ShapeDtypeStruct((M_PROMPT, MLA_HEADS * MLA_KV_LORA), BF16),
        grid=(BATCH, nq, nk),
        in_specs=[
            pl.BlockSpec((tq, MLA_HEADS * MLA_QW), lambda b, qi, ki: (b * nq + qi, 0)),
            pl.BlockSpec((tk, MLA_KW), lambda b, qi, ki: (b * nk + _kv_clamp(qi, ki, tq, tk), 0)),
        ],
        out_specs=pl.BlockSpec((tq, MLA_HEADS * MLA_KV_LORA), lambda b, qi, ki: (b * nq + qi, 0)),
        scratch_shapes=[pltpu.VMEM((MLA_HEADS, tq, 1), F32), pltpu.VMEM((MLA_HEADS, tq, 1), F32),
                        pltpu.VMEM((MLA_HEADS, tq, MLA_KV_LORA), F32)],
        compiler_params=_cparams(("parallel", "parallel", "arbitrary")),
        name="flash_mla",
    )(qcat, mla_bf)


CUM_PAGES = 16


def _fox_cum_decode_kernel(pt_ref, nl_ref, *rest):
    page_refs = rest[:CUM_PAGES]
    d_ref, cn_ref, carry_ref = rest[CUM_PAGES:]
    c = pl.program_id(1)
    rows = CUM_PAGES * FOX_HEADS

    @pl.when(c == 0)
    def _():
        carry_ref[...] = jnp.zeros_like(carry_ref)
        r = lax.broadcasted_iota(jnp.int32, (LANES, LANES), 0)
        s = lax.broadcasted_iota(jnp.int32, (LANES, LANES), 1)
        cn_ref[0] = _dot_exact(nl_ref[0], (r <= s).astype(F32))

    x = jnp.concatenate([p[0, 0] for p in page_refs], axis=0)
    j = lax.broadcasted_iota(jnp.int32, (LANES, LANES), 0)
    s = lax.broadcasted_iota(jnp.int32, (LANES, LANES), 1)
    d_in = _dot_exact(x, (j > s).astype(F32))
    tot = jnp.broadcast_to(jnp.sum(x, axis=-1, keepdims=True), (rows, LANES))
    r0 = lax.broadcasted_iota(jnp.int32, (rows, rows), 0)
    r1 = lax.broadcasted_iota(jnp.int32, (rows, rows), 1)
    later = jnp.logical_and(r1 % FOX_HEADS == r0 % FOX_HEADS, r1 > r0).astype(F32)
    d_pages = _dot_exact(later, tot)
    carry = carry_ref[...]
    d = d_in + d_pages + jnp.concatenate([carry] * CUM_PAGES, axis=0)
    d_ref[0] = d.reshape(CUM_PAGES, FOX_HEADS, LANES)
    h0 = lax.broadcasted_iota(jnp.int32, (FOX_HEADS, rows), 0)
    h1 = lax.broadcasted_iota(jnp.int32, (FOX_HEADS, rows), 1)
    carry_ref[...] = carry + _dot_exact((h1 % FOX_HEADS == h0).astype(F32), tot)


def fox_cum_decode(page_table, logf_t_cache, layer, new_logf_t):
    nch = N_PAGES // CUM_PAGES
    page_spec = lambda p: pl.BlockSpec(
        (1, 1, FOX_HEADS, LANES),
        lambda b, c, pt, p=p: (layer, pt[b, (nch - 1 - c) * CUM_PAGES + p], 0, 0))
    grid_spec = pltpu.PrefetchScalarGridSpec(
        num_scalar_prefetch=1,
        grid=(DEC_BATCH, nch),
        in_specs=[pl.BlockSpec((1, FOX_HEADS, LANES), lambda b, c, pt: (b, 0, 0))]
        + [page_spec(p) for p in range(CUM_PAGES)],
        out_specs=[pl.BlockSpec((1, CUM_PAGES, FOX_HEADS, LANES), lambda b, c, pt: (b, nch - 1 - c, 0, 0)),
                   pl.BlockSpec((1, FOX_HEADS, LANES), lambda b, c, pt: (b, 0, 0))],
        scratch_shapes=[pltpu.VMEM((FOX_HEADS, LANES), F32)],
    )
    return pl.pallas_call(
        _fox_cum_decode_kernel,
        out_shape=[jax.ShapeDtypeStruct((DEC_BATCH, N_PAGES, FOX_HEADS, LANES), F32),
                   jax.ShapeDtypeStruct((DEC_BATCH, FOX_HEADS, LANES), F32)],
        grid_spec=grid_spec,
        compiler_params=_cparams(("parallel", "arbitrary")),
        name="fox_cum_decode",
    )(page_table, new_logf_t, *([logf_t_cache] * CUM_PAGES))


DEC_PAGES = 8
DEC_ROWS = 32


def _decode_kernel(pt_ref, *refs, variant, npages, nch, lam_init):
    it = iter(refs)
    qa_ref = next(it)
    qb_ref = next(it) if variant == "mla" else None
    new_ref = next(it)
    if variant == "fox":
        cn_ref, d_ref = next(it), next(it)
    if variant == "diff":
        lam_ref, subln_ref = next(it), next(it)
    page_refs = [next(it) for _ in range(npages)]
    o_ref, m_ref, l_ref, acc_ref = next(it), next(it), next(it), next(it)
    c = pl.program_id(1)

    def scores(page):
        s = _dot_nt(qa_ref[0], page[:, :2 * LANES])
        if variant == "mla":
            s = s + _dot_nt(qb_ref[0], page[:, MLA_KV_LORA:MLA_KV_LORA + MLA_ROPE])
        return s

    def values(page):
        return page[:, :MLA_KV_LORA] if variant == "mla" else page[:, 2 * LANES:]

    def fox_bias(d_page):
        cn = cn_ref[0]
        return jnp.concatenate(
            [d_page + jnp.broadcast_to(cn[:, t:t + 1], (FOX_HEADS, LANES)) for t in range(DEC_SEQ)], axis=0)

    def update(s_list, v_list):
        s = jnp.concatenate(s_list, axis=1) if len(s_list) > 1 else s_list[0]

        def pv(p):
            out = None
            for i, v in enumerate(v_list):
                t = _dot(p[:, i * PAGE_SIZE:(i + 1) * PAGE_SIZE], v)
                out = t if out is None else out + t
            return out

        _softmax_step(s, pv, m_ref, l_ref, acc_ref, slice(None))

    @pl.when(c == 0)
    def _():
        _softmax_init(m_ref, l_ref, acc_ref)
        page = new_ref[0]
        s = scores(page)
        if variant == "fox":
            s = s + fox_bias(-cn_ref[0])
        row = lax.broadcasted_iota(jnp.int32, (DEC_ROWS, PAGE_SIZE), 0)
        key = lax.broadcasted_iota(jnp.int32, (DEC_ROWS, PAGE_SIZE), 1)
        tok = (row % (DEC_SEQ * DIFF_HEADS)) // DIFF_HEADS if variant == "diff" else row // FOX_HEADS
        s = jnp.where(key <= tok, s, NEG)
        update([s], [values(page)])

    s_list, v_list = [], []
    for i, p_ref in enumerate(page_refs):
        page = p_ref[0, 0].astype(BF16)
        s = scores(page)
        if variant == "fox":
            s = s + fox_bias(d_ref[0, i])
        s_list.append(s)
        v_list.append(values(page))
    update(s_list, v_list)

    @pl.when(c == nch - 1)
    def _():
        o = acc_ref[...] / l_ref[...]
        if variant == "diff":
            half = DEC_ROWS // 2
            lam = _diff_lambda(lam_ref, lam_init)
            o = _diff_finish(o[:half], o[half:], lam, subln_ref[...], lam_init)
        elif variant == "fox":
            row = lax.broadcasted_iota(jnp.int32, (DEC_ROWS, FOX_HD), 0)
            o = jnp.where(row % FOX_HEADS < FOX_GROUP, o[:, :FOX_HD], o[:, FOX_HD:])
        o_ref[0] = o.astype(BF16)


def decode_attention(variant, page_table, cache, layer, qa, new_page, *, qb=None, cn=None, d=None,
                     lam=None, subln=None, lam_init=0.0):
    npages = DEC_PAGES
    nch = N_PAGES // npages
    width = cache.shape[-1]
    per_seq = lambda shape: pl.BlockSpec((1,) + shape, lambda b, c, pt: (b,) + (0,) * len(shape))
    const = lambda shape: pl.BlockSpec(shape, lambda b, c, pt: (0,) * len(shape))
    in_specs = [per_seq((DEC_ROWS, 2 * LANES))]
    args = [qa]
    if variant == "mla":
        in_specs.append(per_seq((DEC_ROWS, MLA_ROPE)))
        args.append(qb)
    in_specs.append(per_seq((PAGE_SIZE, width)))
    args.append(new_page)
    if variant == "fox":
        in_specs += [per_seq((FOX_HEADS, LANES)),
                     pl.BlockSpec((1, npages, FOX_HEADS, LANES), lambda b, c, pt: (b, nch - 1 - c, 0, 0))]
        args += [cn, d]
    if variant == "diff":
        in_specs += [const((4, DIFF_HD)), const((1, 2 * DIFF_HD))]
        args += [lam, subln.reshape(1, -1)]
    for p in range(npages):
        in_specs.append(pl.BlockSpec(
            (1, 1, PAGE_SIZE, width),
            lambda b, c, pt, p=p: (layer, pt[b, (nch - 1 - c) * npages + p], 0, 0)))
        args.append(cache)
    out_rows = DEC_ROWS // 2 if variant == "diff" else DEC_ROWS
    out_w = FOX_HD if variant == "fox" else 2 * LANES
    grid_spec = pltpu.PrefetchScalarGridSpec(
        num_scalar_prefetch=1,
        grid=(DEC_BATCH, nch),
        in_specs=in_specs,
        out_specs=pl.BlockSpec((1, out_rows, out_w), lambda b, c, pt: (b, 0, 0)),
        scratch_shapes=[pltpu.VMEM((DEC_ROWS, 1), F32), pltpu.VMEM((DEC_ROWS, 1), F32),
                        pltpu.VMEM((DEC_ROWS, 2 * LANES), F32)],
    )
    return pl.pallas_call(
        functools.partial(_decode_kernel, variant=variant, npages=npages, nch=nch, lam_init=lam_init),
        out_shape=jax.ShapeDtypeStruct((DEC_BATCH, out_rows, out_w), BF16),
        grid_spec=grid_spec,
        compiler_params=_cparams(("parallel", "arbitrary")),
        name="decode_" + variant,
    )(page_table, *args)


def _rope_tables():
    pos = jnp.concatenate([jnp.tile(jnp.arange(SEQ, dtype=jnp.int32), BATCH),
                           jnp.tile(PAST_LEN + jnp.arange(DEC_SEQ, dtype=jnp.int32), DEC_BATCH)])

    def tab(dim):
        half = dim // 2
        inv_freq = 1.0 / (ROPE_THETA ** (jnp.arange(half, dtype=F32) / half))
        ang = pos.astype(F32)[:, None] * inv_freq[None, :]
        cos, sin = jnp.cos(ang), jnp.sin(ang)
        reps = LANES // dim
        return (jnp.tile(jnp.concatenate([cos, cos], axis=1), (1, reps)),
                jnp.tile(jnp.concatenate([-sin, sin], axis=1), (1, reps)))

    return tab(DIFF_HD) + tab(MLA_ROPE)


def _prep_w_in(w_in):
    sizes = (1024, 256, 256, 1024, 256, 256, FOX_HEADS, MLA_Q_LORA, MLA_KV_LORA + MLA_ROPE, N_BRANCH * D_MODEL)
    offs = np.concatenate([[0], np.cumsum(sizes)])
    dq, dk, dv, fq, fk, fv, ff, mqa, mkva, gate = [w_in[:, offs[i]:offs[i + 1]] for i in range(len(sizes))]
    z = lambda n: jnp.zeros((D_MODEL, n), w_in.dtype)
    cols = [dq, fq, dk, dv, fk, fv, mqa, mkva[:, :MLA_KV_LORA], mkva[:, MLA_KV_LORA:], z(LANES - MLA_ROPE),
            ff, z(LANES - FOX_HEADS), gate]
    return jnp.concatenate(cols, axis=1).astype(BF16)


def _prep_w_qb(w_qb):
    nope = w_qb[:, :, :MLA_NOPE].reshape(MLA_Q_LORA, MLA_HEADS * MLA_NOPE)
    pe = jnp.pad(w_qb[:, :, MLA_NOPE:], ((0, 0), (0, 0), (0, LANES - MLA_ROPE)))
    return jnp.concatenate([nope, pe.reshape(MLA_Q_LORA, MLA_HEADS * LANES)], axis=1).astype(BF16)


def _layer(l, x, tabs, caches, page_table, lw):
    (g_mix_pre, w_in, b_fox_f, b_gate, diff_lambda, diff_subln, g_mla_qa, w_mla_qb, g_mla_kva,
     w_mla_kvb, w_branch, w_o, g_mix_post, g_mlp_pre, w_up, w_down, g_mlp_post) = lw
    cache_diff, cache_fox, cache_logf_t, cache_mla = caches
    c128, s128, c64, s64 = tabs
    lam_init = 0.8 - 0.6 * math.exp(-0.3 * l)

    proj = norm_matmul(x, 0, D_MODEL, g_mix_pre, _prep_w_in(w_in), tm=512, tn=512, name="in_proj")
    b_fox_pad = jnp.pad(b_fox_f, (0, LANES - FOX_HEADS)).reshape(1, LANES)
    (dq, fq, new_diff, dkv_bf, fkv_bf, new_logf, logf_pad, new_mla, mla_bf) = inproj_epilogue(
        proj, tabs, g_mla_kva, b_fox_pad, tm=256)
    new_fox = proj[:, C_FKV:C_FKV + 512]

    q = norm_matmul(proj, C_MQA // MLA_Q_LORA, MLA_Q_LORA, g_mla_qa, _prep_w_qb(w_mla_qb),
                    tm=512, tn=512, name="mla_q_proj")
    w_uk = jnp.transpose(w_mla_kvb[:, :, :MLA_NOPE], (1, 2, 0)).astype(BF16)
    w_uv = jnp.transpose(w_mla_kvb[:, :, MLA_NOPE:], (1, 0, 2)).astype(BF16)
    qcat = mla_q_prep(q, w_uk, c64, s64, tm=256)

    logf_t = jnp.transpose(new_logf[:M_PROMPT].reshape(BATCH, SEQ, FOX_HEADS), (0, 2, 1))
    cum_pad, cum_t = fox_cum_prompt(logf_pad, logf_t)
    o_diff_p = flash_diff(dq, dkv_bf, diff_lambda, diff_subln, lam_init, tq=256, tk=256)
    o_fox_p = flash_fox(fq, fkv_bf, cum_pad, cum_t, tq=256, tk=256)
    o_lat_p = flash_mla(qcat, mla_bf, tq=256, tk=256)

    def new_page(rows):
        r = rows[M_PROMPT:].reshape(DEC_BATCH, DEC_SEQ, rows.shape[-1])
        return jnp.pad(r, ((0, 0), (0, PAGE_SIZE - DEC_SEQ), (0, 0)))

    dq_s = dq[M_PROMPT:].reshape(DEC_BATCH, DEC_SEQ * DIFF_HEADS, 2, DIFF_HD)
    zq = jnp.zeros((DEC_BATCH, DEC_SEQ * DIFF_HEADS, DIFF_HD), BF16)
    qa_diff = jnp.concatenate([jnp.concatenate([dq_s[:, :, 0], zq], axis=-1),
                               jnp.concatenate([zq, dq_s[:, :, 1]], axis=-1)], axis=1)
    o_diff_s = decode_attention("diff", page_table, cache_diff, l, qa_diff, new_page(dkv_bf),
                                lam=diff_lambda, subln=diff_subln, lam_init=lam_init)

    fq_s = fq[M_PROMPT:].reshape(DEC_BATCH, DEC_SEQ, FOX_HEADS, FOX_HD)
    first = (jnp.arange(FOX_HEADS) < FOX_GROUP)[None, None, :, None]
    zf = jnp.zeros_like(fq_s)
    qa_fox = jnp.concatenate([jnp.where(first, fq_s, zf), jnp.where(first, zf, fq_s)], axis=-1)
    qa_fox = qa_fox.reshape(DEC_BATCH, DEC_ROWS, 2 * FOX_HD)
    nl_t = jnp.transpose(new_logf[M_PROMPT:].reshape(DEC_BATCH, DEC_SEQ, FOX_HEADS), (0, 2, 1))
    nl_t = jnp.pad(nl_t, ((0, 0), (0, 0), (0, LANES - DEC_SEQ)))
    d_past, cn = fox_cum_decode(page_table, cache_logf_t, l, nl_t)
    o_fox_s = decode_attention("fox", page_table, cache_fox, l, qa_fox, new_page(fkv_bf), cn=cn, d=d_past)

    qc_s = qcat[M_PROMPT:].reshape(DEC_BATCH, DEC_ROWS, MLA_QW)
    o_lat_s = decode_attention("mla", page_table, cache_mla, l, qc_s[..., :MLA_KV_LORA],
                               new_page(mla_bf[:, :MLA_KV_LORA + MLA_ROPE]),
                               qb=qc_s[..., MLA_KV_LORA:MLA_KV_LORA + MLA_ROPE])

    o_diff = jnp.concatenate([o_diff_p, o_diff_s.reshape(M_SAMPLE, BRANCH_W)], axis=0)
    o_fox = jnp.concatenate([o_fox_p, o_fox_s.reshape(M_SAMPLE, BRANCH_W)], axis=0)
    o_lat = jnp.concatenate([o_lat_p, o_lat_s.reshape(M_SAMPLE, MLA_HEADS * MLA_KV_LORA)], axis=0)
    o_mla = mla_out_proj(o_lat, w_uv, tm=512)
    o_cat = jnp.concatenate([o_diff, o_fox, o_mla], axis=1)

    merged = gated_merge(o_cat, proj, b_gate, w_branch.astype(BF16), tm=512, tn=512)
    x = matmul_norm_res(merged, w_o.astype(BF16), g_mix_post, x, tm=512, tk=1024, name="out_proj")
    h = norm_matmul(x, 0, D_MODEL, g_mlp_pre, w_up.astype(BF16), tm=512, tn=1024, act="relu2",
                    out_dtype=BF16, name="mlp_up")
    x = matmul_norm_res(h, w_down.astype(BF16), g_mlp_post, x, tm=512, tk=1024, name="mlp_down")
    return x, (new_diff, new_fox, new_logf, new_mla)


def kernel(x_prompt, x_sample, cache_diff_kv, cache_fox_kv, cache_fox_logf, cache_mla_kv, page_table,
           g_mix_pre, w_in, b_fox_f, b_gate, diff_lambda, diff_subln, g_mla_qa, w_mla_qb, g_mla_kva,
           w_mla_kvb, w_branch, w_o, g_mix_post, g_mlp_pre, w_up, w_down, g_mlp_post):
    n_phys = cache_diff_kv.shape[1]
    caches = (cache_diff_kv.reshape(DEPTH, n_phys, PAGE_SIZE, 4 * DIFF_HD),
              cache_fox_kv.reshape(DEPTH, n_phys, PAGE_SIZE, 4 * FOX_HD),
              jnp.transpose(cache_fox_logf, (0, 1, 3, 2)),
              cache_mla_kv)
    tabs = _rope_tables()
    x = jnp.concatenate([x_prompt.reshape(M_PROMPT, D_MODEL), x_sample.reshape(M_SAMPLE, D_MODEL)], axis=0)
    states = []
    for l in range(DEPTH):
        lw = (g_mix_pre[l], w_in[l], b_fox_f[l], b_gate[l], diff_lambda[l], diff_subln[l], g_mla_qa[l],
              w_mla_qb[l], g_mla_kva[l], w_mla_kvb[l], w_branch[l], w_o[l], g_mix_post[l], g_mlp_pre[l],
              w_up[l], w_down[l], g_mlp_post[l])
        x, st = _layer(l, x, tabs, caches, page_table, lw)
        states.append(st)

    def out(i, shape_p, shape_s):
        rows = jnp.stack([s[i] for s in states], axis=0)
        return (rows[:, :M_PROMPT].reshape((DEPTH, BATCH, SEQ) + shape_p),
                rows[:, M_PROMPT:].reshape((DEPTH, DEC_BATCH, DEC_SEQ) + shape_s))

    diff_p, diff_s = out(0, (2, 1, 2 * DIFF_HD), (2, 1, 2 * DIFF_HD))
    fox_p, fox_s = out(1, (2, FOX_KV_HEADS, FOX_HD), (2, FOX_KV_HEADS, FOX_HD))
    logf_p, logf_s = out(2, (FOX_HEADS,), (FOX_HEADS,))
    mla_p, mla_s = out(3, (MLA_KV_LORA + MLA_ROPE,), (MLA_KV_LORA + MLA_ROPE,))
    y_p = x[:M_PROMPT].reshape(BATCH, SEQ, D_MODEL)
    y_s = x[M_PROMPT:].reshape(DEC_BATCH, DEC_SEQ, D_MODEL)
    return (y_p, y_s, diff_p, fox_p, logf_p, mla_p, diff_s, fox_s, logf_s, mla_s)
```

```python
import functools
import math

import numpy as np
import jax
import jax.numpy as jnp
from jax import lax
from jax.experimental import pallas as pl
from jax.experimental.pallas import tpu as pltpu

D_MODEL = 2048
BATCH = 4
SEQ = 2048
DEPTH = 2
DEC_BATCH = 128
DEC_SEQ = 4
PAST_LEN = 8192
PAGE_SIZE = 128
N_PAGES = PAST_LEN // PAGE_SIZE
BRANCH_W = D_MODEL // 2
DIFF_HD = 128
DIFF_HEADS = BRANCH_W // (2 * DIFF_HD)
FOX_HD = 128
FOX_HEADS = BRANCH_W // FOX_HD
FOX_KV_HEADS = 2
FOX_GROUP = FOX_HEADS // FOX_KV_HEADS
MLA_V = 128
MLA_HEADS = BRANCH_W // MLA_V
MLA_NOPE = 128
MLA_ROPE = 64
MLA_Q_LORA = D_MODEL // 4
MLA_KV_LORA = D_MODEL // 8
N_BRANCH = 3
D_FF = 4 * D_MODEL
ROPE_THETA = 10000.0
EPS = 1e-6

M_PROMPT = BATCH * SEQ
M_SAMPLE = DEC_BATCH * DEC_SEQ
M_ALL = M_PROMPT + M_SAMPLE

LANES = 128
NEG = -1e30
VMEM_LIMIT = 48 * 1024 * 1024

C_DQ = 0
C_FQ = 1024
C_DKV = 2048
C_FKV = 2560
C_MQA = 3072
C_MKV = 3584
C_GATE = 4096
IN_COLS_PAD = C_GATE + N_BRANCH * D_MODEL
MLA_QW = MLA_KV_LORA + LANES
MLA_KW = MLA_KV_LORA + LANES

BF16 = jnp.bfloat16
F32 = jnp.float32


def _cparams(sem):
    return pltpu.CompilerParams(dimension_semantics=sem, vmem_limit_bytes=VMEM_LIMIT)


def _dot(a, b):
    return jnp.dot(a, b, preferred_element_type=F32)


def _dot_nt(a, b):
    return lax.dot_general(a, b, (((1,), (1,)), ((), ())), preferred_element_type=F32)


def _dot_exact(a, b):
    return jnp.dot(a, b, preferred_element_type=F32, precision=lax.Precision.HIGHEST)


def _rms(x, g):
    ms = jnp.mean(x * x, axis=-1, keepdims=True)
    return x * lax.rsqrt(ms + EPS) * g


def _norm_matmul_kernel(x_ref, g_ref, w_ref, o_ref, u_ref, *, act):
    @pl.when(pl.program_id(1) == 0)
    def _():
        u_ref[...] = _rms(x_ref[...].astype(F32), g_ref[...]).astype(BF16)

    acc = _dot(u_ref[...], w_ref[...])
    if act == "relu2":
        acc = jnp.square(jnp.maximum(acc, 0.0))
    o_ref[...] = acc.astype(o_ref.dtype)


def norm_matmul(x, x_col, k, g, w, *, tm, tn, act=None, out_dtype=F32, name):
    m = x.shape[0]
    n = w.shape[1]
    return pl.pallas_call(
        functools.partial(_norm_matmul_kernel, act=act),
        out_shape=jax.ShapeDtypeStruct((m, n), out_dtype),
        grid=(m // tm, n // tn),
        in_specs=[
            pl.BlockSpec((tm, k), lambda i, j: (i, x_col)),
            pl.BlockSpec((1, k), lambda i, j: (0, 0)),
            pl.BlockSpec((k, tn), lambda i, j: (0, j)),
        ],
        out_specs=pl.BlockSpec((tm, tn), lambda i, j: (i, j)),
        scratch_shapes=[pltpu.VMEM((tm, k), BF16)],
        compiler_params=_cparams(("parallel", "arbitrary")),
        name=name,
    )(x, g.reshape(1, k), w)


def _matmul_norm_res_kernel(a_ref, w_ref, g_ref, r_ref, o_ref, acc_ref, *, nk):
    k = pl.program_id(1)

    @pl.when(k == 0)
    def _():
        acc_ref[...] = jnp.zeros_like(acc_ref)

    acc_ref[...] += _dot(a_ref[...], w_ref[...])

    @pl.when(k == nk - 1)
    def _():
        o_ref[...] = r_ref[...] + _rms(acc_ref[...], g_ref[...])


def matmul_norm_res(a, w, g, resid, *, tm, tk, name):
    m, kdim = a.shape
    n = w.shape[1]
    nk = kdim // tk
    return pl.pallas_call(
        functools.partial(_matmul_norm_res_kernel, nk=nk),
        out_shape=jax.ShapeDtypeStruct((m, n), F32),
        grid=(m // tm, nk),
        in_specs=[
            pl.BlockSpec((tm, tk), lambda i, k: (i, k)),
            pl.BlockSpec((tk, n), lambda i, k: (k, 0)),
            pl.BlockSpec((1, n), lambda i, k: (0, 0)),
            pl.BlockSpec((tm, n), lambda i, k: (i, 0)),
        ],
        out_specs=pl.BlockSpec((tm, n), lambda i, k: (i, 0)),
        scratch_shapes=[pltpu.VMEM((tm, n), F32)],
        compiler_params=_cparams(("parallel", "arbitrary")),
        name=name,
    )(a, w, g.reshape(1, n), resid)


def _merge_kernel(o_ref, g0_ref, g1_ref, g2_ref, b_ref, wb_ref, m_ref):
    acc = None
    for n, g_ref in enumerate((g0_ref, g1_ref, g2_ref)):
        y = _dot(o_ref[:, n * BRANCH_W:(n + 1) * BRANCH_W], wb_ref[n])
        term = jax.nn.sigmoid(g_ref[...] + b_ref[n]) * y
        acc = term if acc is None else acc + term
    m_ref[...] = acc.astype(m_ref.dtype)


def gated_merge(o_cat, proj, b_gate, w_branch, *, tm, tn):
    m = o_cat.shape[0]
    nj = D_MODEL // tn
    gate_spec = lambda n: pl.BlockSpec(
        (tm, tn), lambda i, j, n=n: (i, (C_GATE + n * D_MODEL) // tn + j))
    return pl.pallas_call(
        _merge_kernel,
        out_shape=jax.ShapeDtypeStruct((m, D_MODEL), BF16),
        grid=(m // tm, nj),
        in_specs=[
            pl.BlockSpec((tm, N_BRANCH * BRANCH_W), lambda i, j: (i, 0)),
            gate_spec(0), gate_spec(1), gate_spec(2),
            pl.BlockSpec((N_BRANCH, 1, tn), lambda i, j: (0, 0, j)),
            pl.BlockSpec((N_BRANCH, BRANCH_W, tn), lambda i, j: (0, 0, j)),
        ],
        out_specs=pl.BlockSpec((tm, tn), lambda i, j: (i, j)),
        compiler_params=_cparams(("parallel", "arbitrary")),
        name="gated_merge",
    )(o_cat, proj, proj, proj, b_gate.reshape(N_BRANCH, 1, D_MODEL), w_branch)


def _rope128(x, cos_f, sin_s):
    return x * cos_f + pltpu.roll(x, DIFF_HD // 2, 1) * sin_s


def _rope64(x, cos_f, sin_s):
    lane = lax.broadcasted_iota(jnp.int32, x.shape, 1)
    half = MLA_ROPE // 2
    rot = jnp.where((lane % MLA_ROPE) < half,
                    pltpu.roll(x, LANES - half, 1), pltpu.roll(x, half, 1))
    return x * cos_f + rot * sin_s


def _log_sigmoid(x):
    return jnp.minimum(x, 0.0) - jnp.log1p(jnp.exp(-jnp.abs(x)))


def _inproj_epilogue_kernel(dq_ref, fq_ref, dkv_ref, fkv_ref, mkv_ref, c128_ref, s128_ref,
                            c64_ref, s64_ref, gkv_ref, bf_ref,
                            dq_o, fq_o, dkv_o, dkv_bf_o, fkv_bf_o, logf_o, logfp_o, mla_o, mla_bf_o):
    cos_f, sin_s = c128_ref[...], s128_ref[...]
    qscale = DIFF_HD ** -0.5
    for c in range(2 * DIFF_HEADS):
        sl = slice(c * DIFF_HD, (c + 1) * DIFF_HD)
        dq_o[:, sl] = (_rope128(dq_ref[:, sl], cos_f, sin_s) * qscale).astype(BF16)
    fq_o[...] = (fq_ref[...] * (FOX_HD ** -0.5)).astype(BF16)
    for c in range(2):
        sl = slice(c * DIFF_HD, (c + 1) * DIFF_HD)
        kr = _rope128(dkv_ref[:, sl], cos_f, sin_s)
        dkv_o[:, sl] = kr
        dkv_bf_o[:, sl] = kr.astype(BF16)
    v = dkv_ref[:, 2 * DIFF_HD:]
    dkv_o[:, 2 * DIFF_HD:] = v
    dkv_bf_o[:, 2 * DIFF_HD:] = v.astype(BF16)
    fkv_bf_o[...] = fkv_ref[...].astype(BF16)
    lf = _log_sigmoid(mkv_ref[:, MLA_KV_LORA + LANES:] + bf_ref[...])
    logfp_o[...] = lf
    logf_o[...] = lf[:, :FOX_HEADS]
    ckv = _rms(mkv_ref[:, :MLA_KV_LORA], gkv_ref[...])
    kpe = _rope64(mkv_ref[:, MLA_KV_LORA:MLA_KV_LORA + LANES], c64_ref[...], s64_ref[...])
    mla_o[:, :MLA_KV_LORA] = ckv
    mla_o[:, MLA_KV_LORA:] = kpe[:, :MLA_ROPE]
    mla_bf_o[:, :MLA_KV_LORA] = ckv.astype(BF16)
    mla_bf_o[:, MLA_KV_LORA:] = kpe.astype(BF16)


def inproj_epilogue(proj, tabs, g_kva, b_fox_pad, *, tm):
    m = proj.shape[0]
    c128, s128, c64, s64 = tabs
    row = lambda w, c: pl.BlockSpec((tm, w), lambda i, c=c: (i, c))
    full = lambda w: pl.BlockSpec((1, w), lambda i: (0, 0))
    outs = [
        ((m, 1024), BF16), ((m, 1024), BF16), ((m, 512), F32), ((m, 512), BF16), ((m, 512), BF16),
        ((m, FOX_HEADS), F32), ((m, LANES), F32), ((m, MLA_KV_LORA + MLA_ROPE), F32), ((m, MLA_KW), BF16),
    ]
    return pl.pallas_call(
        _inproj_epilogue_kernel,
        out_shape=[jax.ShapeDtypeStruct(s, d) for s, d in outs],
        grid=(m // tm,),
        in_specs=[
            row(1024, C_DQ // 1024), row(1024, C_FQ // 1024), row(512, C_DKV // 512),
            row(512, C_FKV // 512), row(512, C_MKV // 512),
            row(LANES, 0), row(LANES, 0), row(LANES, 0), row(LANES, 0),
            full(MLA_KV_LORA), full(LANES),
        ],
        out_specs=[pl.BlockSpec((tm, s[1]), lambda i: (i, 0)) for s, _ in outs],
        compiler_params=_cparams(("parallel",)),
        name="inproj_epilogue",
    )(proj, proj, proj, proj, proj, c128, s128, c64, s64, g_kva.reshape(1, -1), b_fox_pad)


def _mla_q_kernel(q_ref, wuk_ref, c64_ref, s64_ref, o_ref):
    scale = (MLA_NOPE + MLA_ROPE) ** -0.5
    for h in range(MLA_HEADS):
        qn = q_ref[:, h * MLA_NOPE:(h + 1) * MLA_NOPE].astype(BF16)
        qlat = _dot(qn, wuk_ref[h])
        base = MLA_HEADS * MLA_NOPE + h * LANES
        qpe = _rope64(q_ref[:, base:base + LANES], c64_ref[...], s64_ref[...])
        o_ref[:, h * MLA_QW:h * MLA_QW + MLA_KV_LORA] = (qlat * scale).astype(BF16)
        o_ref[:, h * MLA_QW + MLA_KV_LORA:(h + 1) * MLA_QW] = (qpe * scale).astype(BF16)


def mla_q_prep(q, w_uk, c64, s64, *, tm):
    m, qw = q.shape
    return pl.pallas_call(
        _mla_q_kernel,
        out_shape=jax.ShapeDtypeStruct((m, MLA_HEADS * MLA_QW), BF16),
        grid=(m // tm,),
        in_specs=[
            pl.BlockSpec((tm, qw), lambda i: (i, 0)),
            pl.BlockSpec((MLA_HEADS, MLA_NOPE, MLA_KV_LORA), lambda i: (0, 0, 0)),
            pl.BlockSpec((tm, LANES), lambda i: (i, 0)),
            pl.BlockSpec((tm, LANES), lambda i: (i, 0)),
        ],
        out_specs=pl.BlockSpec((tm, MLA_HEADS * MLA_QW), lambda i: (i, 0)),
        compiler_params=_cparams(("parallel",)),
        name="mla_q_prep",
    )(q, w_uk, c64, s64)


def _mla_out_kernel(o_ref, wuv_ref, y_ref):
    for h in range(MLA_HEADS):
        y = _dot(o_ref[:, h * MLA_KV_LORA:(h + 1) * MLA_KV_LORA], wuv_ref[h])
        y_ref[:, h * MLA_V:(h + 1) * MLA_V] = y.astype(BF16)


def mla_out_proj(o_lat, w_uv, *, tm):
    m = o_lat.shape[0]
    return pl.pallas_call(
        _mla_out_kernel,
        out_shape=jax.ShapeDtypeStruct((m, BRANCH_W), BF16),
        grid=(m // tm,),
        in_specs=[
            pl.BlockSpec((tm, MLA_HEADS * MLA_KV_LORA), lambda i: (i, 0)),
            pl.BlockSpec((MLA_HEADS, MLA_KV_LORA, MLA_V), lambda i: (0, 0, 0)),
        ],
        out_specs=pl.BlockSpec((tm, BRANCH_W), lambda i: (i, 0)),
        compiler_params=_cparams(("parallel",)),
        name="mla_out_proj",
    )(o_lat, w_uv)


CUM_BLK = 256


def _fox_cum_prompt_kernel(x_ref, xt_ref, cum_ref, cumt_ref):
    r = lax.broadcasted_iota(jnp.int32, (CUM_BLK, CUM_BLK), 0)
    c = lax.broadcasted_iota(jnp.int32, (CUM_BLK, CUM_BLK), 1)
    lower = (c <= r).astype(F32)
    upper = (r <= c).astype(F32)
    carry = jnp.zeros((1, LANES), F32)
    carry_t = jnp.zeros((FOX_HEADS, 1), F32)
    for b in range(SEQ // CUM_BLK):
        sl = slice(b * CUM_BLK, (b + 1) * CUM_BLK)
        cb = _dot_exact(lower, x_ref[sl, :]) + carry
        cum_ref[sl, :] = cb
        carry = cb[CUM_BLK - 1:CUM_BLK, :]
        cbt = _dot_exact(xt_ref[0, :, sl], upper) + carry_t
        cumt_ref[0, :, sl] = cbt
        carry_t = cbt[:, CUM_BLK - 1:CUM_BLK]


def fox_cum_prompt(logf_pad, logf_t):
    return pl.pallas_call(
        _fox_cum_prompt_kernel,
        out_shape=[jax.ShapeDtypeStruct((M_PROMPT, LANES), F32),
                   jax.ShapeDtypeStruct((BATCH, FOX_HEADS, SEQ), F32)],
        grid=(BATCH,),
        in_specs=[pl.BlockSpec((SEQ, LANES), lambda b: (b, 0)),
                  pl.BlockSpec((1, FOX_HEADS, SEQ), lambda b: (b, 0, 0))],
        out_specs=[pl.BlockSpec((SEQ, LANES), lambda b: (b, 0)),
                   pl.BlockSpec((1, FOX_HEADS, SEQ), lambda b: (b, 0, 0))],
        compiler_params=_cparams(("parallel",)),
        name="fox_cum_prompt",
    )(logf_pad, logf_t)


def _softmax_init(m_ref, l_ref, acc_ref):
    m_ref[...] = jnp.full_like(m_ref, NEG)
    l_ref[...] = jnp.zeros_like(l_ref)
    acc_ref[...] = jnp.zeros_like(acc_ref)


def _softmax_step(s, pv_fn, m_ref, l_ref, acc_ref, idx):
    m_prev = m_ref[idx]
    m_new = jnp.maximum(m_prev, jnp.max(s, axis=-1, keepdims=True))
    alpha = jnp.exp(m_prev - m_new)
    p = jnp.exp(s - m_new)
    l_ref[idx] = alpha * l_ref[idx] + jnp.sum(p, axis=-1, keepdims=True)
    acc_ref[idx] = alpha * acc_ref[idx] + pv_fn(p.astype(BF16))
    m_ref[idx] = m_new


def _causal_mask(qi, ki, tq, tk):
    qpos = qi * tq + lax.broadcasted_iota(jnp.int32, (tq, tk), 0)
    kpos = ki * tk + lax.broadcasted_iota(jnp.int32, (tq, tk), 1)
    return kpos <= qpos


def _diff_lambda(lam_ref, lam_init):
    lf = lam_ref[...]
    a = jnp.sum(lf[0:1] * lf[1:2], axis=-1, keepdims=True)
    b = jnp.sum(lf[2:3] * lf[3:4], axis=-1, keepdims=True)
    return jnp.exp(a) - jnp.exp(b) + lam_init


def _diff_finish(o1, o2, lam, subln, lam_init):
    return _rms(o1 - lam * o2, subln) * (1.0 - lam_init)


def _flash_diff_kernel(q_ref, kv_ref, lam_ref, subln_ref, o_ref, m_ref, l_ref, acc_ref,
                       *, tq, tk, nk, lam_init):
    qi, ki = pl.program_id(1), pl.program_id(2)

    @pl.when(ki == 0)
    def _():
        _softmax_init(m_ref, l_ref, acc_ref)

    @pl.when(ki * tk <= qi * tq + tq - 1)
    def _():
        mask = _causal_mask(qi, ki, tq, tk)
        v = kv_ref[:, 2 * DIFF_HD:]
        for c in range(2 * DIFF_HEADS):
            mp = c % 2
            s = _dot_nt(q_ref[:, c * DIFF_HD:(c + 1) * DIFF_HD], kv_ref[:, mp * DIFF_HD:(mp + 1) * DIFF_HD])
            s = jnp.where(mask, s, NEG)
            _softmax_step(s, lambda p: _dot(p, v), m_ref, l_ref, acc_ref, c)

    @pl.when(ki == nk - 1)
    def _():
        lam = _diff_lambda(lam_ref, lam_init)
        for h in range(DIFF_HEADS):
            o1 = acc_ref[2 * h] / l_ref[2 * h]
            o2 = acc_ref[2 * h + 1] / l_ref[2 * h + 1]
            o = _diff_finish(o1, o2, lam, subln_ref[...], lam_init)
            o_ref[:, h * 2 * DIFF_HD:(h + 1) * 2 * DIFF_HD] = o.astype(BF16)


def _kv_clamp(qi, ki, tq, tk):
    return jnp.minimum(ki, (qi * tq + tq - 1) // tk)


def flash_diff(dq, dkv_bf, diff_lambda, subln, lam_init, *, tq, tk):
    nq, nk = SEQ // tq, SEQ // tk
    return pl.pallas_call(
        functools.partial(_flash_diff_kernel, tq=tq, tk=tk, nk=nk, lam_init=lam_init),
        out_shape=jax.ShapeDtypeStruct((M_PROMPT, BRANCH_W), BF16),
        grid=(BATCH, nq, nk),
        in_specs=[
            pl.BlockSpec((tq, BRANCH_W), lambda b, qi, ki: (b * nq + qi, 0)),
            pl.BlockSpec((tk, 512), lambda b, qi, ki: (b * nk + _kv_clamp(qi, ki, tq, tk), 0)),
            pl.BlockSpec((4, DIFF_HD), lambda b, qi, ki: (0, 0)),
            pl.BlockSpec((1, 2 * DIFF_HD), lambda b, qi, ki: (0, 0)),
        ],
        out_specs=pl.BlockSpec((tq, BRANCH_W), lambda b, qi, ki: (b * nq + qi, 0)),
        scratch_shapes=[pltpu.VMEM((2 * DIFF_HEADS, tq, 1), F32), pltpu.VMEM((2 * DIFF_HEADS, tq, 1), F32),
                        pltpu.VMEM((2 * DIFF_HEADS, tq, 2 * DIFF_HD), F32)],
        compiler_params=_cparams(("parallel", "parallel", "arbitrary")),
        name="flash_diff",
    )(dq, dkv_bf, diff_lambda, subln.reshape(1, -1))


def _flash_fox_kernel(q_ref, k_ref, v_ref, cq_ref, ck_ref, o_ref, m_ref, l_ref, acc_ref, *, tq, tk, nk):
    g, qi, ki = pl.program_id(1), pl.program_id(2), pl.program_id(3)

    @pl.when(ki == 0)
    def _():
        _softmax_init(m_ref, l_ref, acc_ref)

    @pl.when(ki * tk <= qi * tq + tq - 1)
    def _():
        mask = _causal_mask(qi, ki, tq, tk)
        k, v = k_ref[...], v_ref[...]
        lane = lax.broadcasted_iota(jnp.int32, (tq, LANES), 1)
        sub = lax.broadcasted_iota(jnp.int32, (FOX_HEADS, tk), 0)
        for hl in range(FOX_GROUP):
            h = g * FOX_GROUP + hl
            cq = jnp.sum(jnp.where(lane == h, cq_ref[...], 0.0), axis=-1, keepdims=True)
            ck = jnp.sum(jnp.where(sub == h, ck_ref[0], 0.0), axis=0, keepdims=True)
            s = _dot_nt(q_ref[:, hl * FOX_HD:(hl + 1) * FOX_HD], k) + cq - ck
            s = jnp.where(mask, s, NEG)
            _softmax_step(s, lambda p: _dot(p, v), m_ref, l_ref, acc_ref, hl)

    @pl.when(ki == nk - 1)
    def _():
        for hl in range(FOX_GROUP):
            o_ref[:, hl * FOX_HD:(hl + 1) * FOX_HD] = (acc_ref[hl] / l_ref[hl]).astype(BF16)


def flash_fox(fq, fkv_bf, cum_pad, cum_t, *, tq, tk):
    nq, nk = SEQ // tq, SEQ // tk
    gw = FOX_GROUP * FOX_HD
    return pl.pallas_call(
        functools.partial(_flash_fox_kernel, tq=tq, tk=tk, nk=nk),
        out_shape=jax.ShapeDtypeStruct((M_PROMPT, BRANCH_W), BF16),
        grid=(BATCH, FOX_KV_HEADS, nq, nk),
        in_specs=[
            pl.BlockSpec((tq, gw), lambda b, g, qi, ki: (b * nq + qi, g)),
            pl.BlockSpec((tk, FOX_HD), lambda b, g, qi, ki: (b * nk + _kv_clamp(qi, ki, tq, tk), g)),
            pl.BlockSpec((tk, FOX_HD), lambda b, g, qi, ki: (b * nk + _kv_clamp(qi, ki, tq, tk), FOX_KV_HEADS + g)),
            pl.BlockSpec((tq, LANES), lambda b, g, qi, ki: (b * nq + qi, 0)),
            pl.BlockSpec((1, FOX_HEADS, tk), lambda b, g, qi, ki: (b, 0, _kv_clamp(qi, ki, tq, tk))),
        ],
        out_specs=pl.BlockSpec((tq, gw), lambda b, g, qi, ki: (b * nq + qi, g)),
        scratch_shapes=[pltpu.VMEM((FOX_GROUP, tq, 1), F32), pltpu.VMEM((FOX_GROUP, tq, 1), F32),
                        pltpu.VMEM((FOX_GROUP, tq, FOX_HD), F32)],
        compiler_params=_cparams(("parallel", "parallel", "parallel", "arbitrary")),
        name="flash_fox",
    )(fq, fkv_bf, fkv_bf, cum_pad, cum_t)


def _flash_mla_kernel(q_ref, kv_ref, o_ref, m_ref, l_ref, acc_ref, *, tq, tk, nk):
    qi, ki = pl.program_id(1), pl.program_id(2)

    @pl.when(ki == 0)
    def _():
        _softmax_init(m_ref, l_ref, acc_ref)

    @pl.when(ki * tk <= qi * tq + tq - 1)
    def _():
        mask = _causal_mask(qi, ki, tq, tk)
        kv = kv_ref[...]
        v = kv_ref[:, :MLA_KV_LORA]
        for h in range(MLA_HEADS):
            s = _dot_nt(q_ref[:, h * MLA_QW:(h + 1) * MLA_QW], kv)
            s = jnp.where(mask, s, NEG)
            _softmax_step(s, lambda p: _dot(p, v), m_ref, l_ref, acc_ref, h)

    @pl.when(ki == nk - 1)
    def _():
        for h in range(MLA_HEADS):
            o_ref[:, h * MLA_KV_LORA:(h + 1) * MLA_KV_LORA] = (acc_ref[h] / l_ref[h]).astype(BF16)


def flash_mla(qcat, mla_bf, *, tq, tk):
    nq, nk = SEQ // tq, SEQ // tk
    return pl.pallas_call(
        functools.partial(_flash_mla_kernel, tq=tq, tk=tk, nk=nk),
        out_shape=jax.ShapeDtypeStruct((M_PROMPT, MLA_HEADS * MLA_KV_LORA), BF16),
        grid=(BATCH, nq, nk),
        in_specs=[
            pl.BlockSpec((tq, MLA_HEADS * MLA_QW), lambda b, qi, ki: (b * nq + qi, 0)),
            pl.BlockSpec((tk, MLA_KW), lambda b, qi, ki: (b * nk + _kv_clamp(qi, ki, tq, tk), 0)),
        ],
        out_specs=pl.BlockSpec((tq, MLA_HEADS * MLA_KV_LORA), lambda b, qi, ki: (b * nq + qi, 0)),
        scratch_shapes=[pltpu.VMEM((MLA_HEADS, tq, 1), F32), pltpu.VMEM((MLA_HEADS, tq, 1), F32),
                        pltpu.VMEM((MLA_HEADS, tq, MLA_KV_LORA), F32)],
        compiler_params=_cparams(("parallel", "parallel", "arbitrary")),
        name="flash_mla",
    )(qcat, mla_bf)


def _split3(x):
    hi = x.astype(BF16)
    r = x - hi.astype(F32)
    mid = r.astype(BF16)
    lo = (r - mid.astype(F32)).astype(BF16)
    return hi, mid, lo


def _fox_cum_decode_kernel(pt_ref, nl_ref, *rest):
    page_refs = rest[:N_PAGES]
    d_ref, cn_ref = rest[N_PAGES:]
    rows = N_PAGES * FOX_HEADS
    j = lax.broadcasted_iota(jnp.int32, (LANES, LANES), 0)
    s = lax.broadcasted_iota(jnp.int32, (LANES, LANES), 1)
    incl = (j <= s).astype(F32).astype(BF16)
    strict = (j > s).astype(F32).astype(BF16)
    hi, mid, lo = _split3(nl_ref[0])
    cn_ref[0] = _dot(hi, incl) + _dot(mid, incl) + _dot(lo, incl)

    x = jnp.concatenate([p[0, 0] for p in page_refs], axis=0)
    hi, mid, lo = _split3(x)
    d_in = _dot(hi, strict) + _dot(mid, strict) + _dot(lo, strict)
    tot = jnp.broadcast_to(jnp.sum(x, axis=-1, keepdims=True), (rows, LANES))
    r0 = lax.broadcasted_iota(jnp.int32, (rows, rows), 0)
    r1 = lax.broadcasted_iota(jnp.int32, (rows, rows), 1)
    later = jnp.logical_and((r1 - r0) % FOX_HEADS == 0, r1 > r0).astype(F32).astype(BF16)
    hi, mid, lo = _split3(tot)
    d_pages = _dot(later, hi) + _dot(later, mid) + _dot(later, lo)
    d_ref[0] = (d_in + d_pages).reshape(N_PAGES, FOX_HEADS, LANES)


def fox_cum_decode(page_table, logf_t_cache, layer, new_logf_t):
    page_spec = lambda p: pl.BlockSpec(
        (1, 1, FOX_HEADS, LANES), lambda b, pt, p=p: (layer, pt[b, p], 0, 0))
    grid_spec = pltpu.PrefetchScalarGridSpec(
        num_scalar_prefetch=1,
        grid=(DEC_BATCH,),
        in_specs=[pl.BlockSpec((1, FOX_HEADS, LANES), lambda b, pt: (b, 0, 0))]
        + [page_spec(p) for p in range(N_PAGES)],
        out_specs=[pl.BlockSpec((1, N_PAGES, FOX_HEADS, LANES), lambda b, pt: (b, 0, 0, 0)),
                   pl.BlockSpec((1, FOX_HEADS, LANES), lambda b, pt: (b, 0, 0))],
    )
    return pl.pallas_call(
        _fox_cum_decode_kernel,
        out_shape=[jax.ShapeDtypeStruct((DEC_BATCH, N_PAGES, FOX_HEADS, LANES), F32),
                   jax.ShapeDtypeStruct((DEC_BATCH, FOX_HEADS, LANES), F32)],
        grid_spec=grid_spec,
        compiler_params=_cparams(("parallel",)),
        name="fox_cum_decode",
    )(page_table, new_logf_t, *([logf_t_cache] * N_PAGES))


DEC_ROWS = 32
DEC_KEYS = (N_PAGES + 1) * PAGE_SIZE
KV_PARTS = 4


def _decode_kernel(pt_ref, *refs, variant, lam_init):
    it = iter(refs)
    qa_ref = next(it)
    qb_ref = next(it) if variant == "mla" else None
    new_ref = next(it)
    if variant == "fox":
        cn_ref, d_ref = next(it), next(it)
    if variant == "diff":
        lam_ref, subln_ref = next(it), next(it)
    page_refs = [next(it) for _ in range(N_PAGES)]
    o_ref, s_ref, p_ref = next(it), next(it), next(it)
    qa = qa_ref[0]

    def rd(p, idx):
        return page_refs[p][(0, 0) + idx] if p < N_PAGES else new_ref[(0,) + idx]

    def part(p, j):
        return rd(p, (pl.ds(j, PAGE_SIZE, stride=KV_PARTS), slice(None)))

    def scores(p):
        if variant == "mla":
            kt = rd(p, (slice(0, MLA_KV_LORA), slice(None))).astype(BF16)
            pet = rd(p, (slice(MLA_KV_LORA, MLA_KV_LORA + MLA_ROPE), slice(None))).astype(BF16)
            return _dot(qa, kt) + _dot(qb_ref[0], pet)
        k = jnp.concatenate([part(p, 0), part(p, 1)], axis=1).astype(BF16)
        return _dot_nt(qa, k)

    def weighted_values(pm, p):
        if variant == "mla":
            return _dot_nt(pm, rd(p, (slice(0, MLA_KV_LORA), slice(None))).astype(BF16))
        v = jnp.concatenate([part(p, 2), part(p, 3)], axis=1).astype(BF16)
        return _dot(pm, v)

    if variant == "fox":
        cn = cn_ref[0]
        cn_t = [jnp.broadcast_to(cn[:, t:t + 1], (FOX_HEADS, LANES)) for t in range(DEC_SEQ)]

    for p in range(N_PAGES + 1):
        s = scores(p)
        if variant == "fox":
            d_page = d_ref[0, p] if p < N_PAGES else -cn
            s = s + jnp.concatenate([d_page + c for c in cn_t], axis=0)
        if p == N_PAGES:
            row = lax.broadcasted_iota(jnp.int32, (DEC_ROWS, PAGE_SIZE), 0)
            key = lax.broadcasted_iota(jnp.int32, (DEC_ROWS, PAGE_SIZE), 1)
            tok = (row % (DEC_SEQ * DIFF_HEADS)) // DIFF_HEADS if variant == "diff" else row // FOX_HEADS
            s = jnp.where(key <= tok, s, NEG)
        s_ref[:, p * PAGE_SIZE:(p + 1) * PAGE_SIZE] = s

    s = s_ref[...]
    e = jnp.exp(s - jnp.max(s, axis=-1, keepdims=True))
    l = jnp.sum(e, axis=-1, keepdims=True)
    p_ref[...] = e.astype(BF16)

    acc = None
    for p in range(N_PAGES + 1):
        t = weighted_values(p_ref[:, p * PAGE_SIZE:(p + 1) * PAGE_SIZE], p)
        acc = t if acc is None else acc + t

    o = acc / l
    if variant == "diff":
        half = DEC_ROWS // 2
        lam = _diff_lambda(lam_ref, lam_init)
        o = _diff_finish(o[:half], o[half:], lam, subln_ref[...], lam_init)
    elif variant == "fox":
        row = lax.broadcasted_iota(jnp.int32, (DEC_ROWS, FOX_HD), 0)
        o = jnp.where(row % FOX_HEADS < FOX_GROUP, o[:, :FOX_HD], o[:, FOX_HD:])
    o_ref[0] = o.astype(BF16)


def decode_attention(variant, page_table, cache, layer, qa, new_page, *, qb=None, cn=None, d=None,
                     lam=None, subln=None, lam_init=0.0):
    page_shape = tuple(cache.shape[2:])
    per_seq = lambda shape: pl.BlockSpec((1,) + shape, lambda b, pt: (b,) + (0,) * len(shape))
    const = lambda shape: pl.BlockSpec(shape, lambda b, pt: (0,) * len(shape))
    in_specs = [per_seq((DEC_ROWS, 2 * LANES))]
    args = [qa]
    if variant == "mla":
        in_specs.append(per_seq((DEC_ROWS, MLA_ROPE)))
        args.append(qb)
    in_specs.append(per_seq(page_shape))
    args.append(new_page)
    if variant == "fox":
        in_specs += [per_seq((FOX_HEADS, LANES)), per_seq((N_PAGES, FOX_HEADS, LANES))]
        args += [cn, d]
    if variant == "diff":
        in_specs += [const((4, DIFF_HD)), const((1, 2 * DIFF_HD))]
        args += [lam, subln.reshape(1, -1)]
    for p in range(N_PAGES):
        in_specs.append(pl.BlockSpec((1, 1) + page_shape, lambda b, pt, p=p: (layer, pt[b, p], 0, 0)))
        args.append(cache)
    out_rows = DEC_ROWS // 2 if variant == "diff" else DEC_ROWS
    out_w = FOX_HD if variant == "fox" else 2 * LANES
    grid_spec = pltpu.PrefetchScalarGridSpec(
        num_scalar_prefetch=1,
        grid=(DEC_BATCH,),
        in_specs=in_specs,
        out_specs=pl.BlockSpec((1, out_rows, out_w), lambda b, pt: (b, 0, 0)),
        scratch_shapes=[pltpu.VMEM((DEC_ROWS, DEC_KEYS), F32), pltpu.VMEM((DEC_ROWS, DEC_KEYS), BF16)],
    )
    return pl.pallas_call(
        functools.partial(_decode_kernel, variant=variant, lam_init=lam_init),
        out_shape=jax.ShapeDtypeStruct((DEC_BATCH, out_rows, out_w), BF16),
        grid_spec=grid_spec,
        compiler_params=_cparams(("parallel",)),
        name="decode_" + variant,
    )(page_table, *args)


def _rope_tables():
    pos = jnp.concatenate([jnp.tile(jnp.arange(SEQ, dtype=jnp.int32), BATCH),
                           jnp.tile(PAST_LEN + jnp.arange(DEC_SEQ, dtype=jnp.int32), DEC_BATCH)])

    def tab(dim):
        half = dim // 2
        inv_freq = 1.0 / (ROPE_THETA ** (jnp.arange(half, dtype=F32) / half))
        ang = pos.astype(F32)[:, None] * inv_freq[None, :]
        cos, sin = jnp.cos(ang), jnp.sin(ang)
        reps = LANES // dim
        return (jnp.tile(jnp.concatenate([cos, cos], axis=1), (1, reps)),
                jnp.tile(jnp.concatenate([-sin, sin], axis=1), (1, reps)))

    return tab(DIFF_HD) + tab(MLA_ROPE)


def _prep_w_in(w_in):
    sizes = (1024, 256, 256, 1024, 256, 256, FOX_HEADS, MLA_Q_LORA, MLA_KV_LORA + MLA_ROPE, N_BRANCH * D_MODEL)
    offs = np.concatenate([[0], np.cumsum(sizes)])
    dq, dk, dv, fq, fk, fv, ff, mqa, mkva, gate = [w_in[:, offs[i]:offs[i + 1]] for i in range(len(sizes))]
    z = lambda n: jnp.zeros((D_MODEL, n), w_in.dtype)
    cols = [dq, fq, dk, dv, fk, fv, mqa, mkva[:, :MLA_KV_LORA], mkva[:, MLA_KV_LORA:], z(LANES - MLA_ROPE),
            ff, z(LANES - FOX_HEADS), gate]
    return jnp.concatenate(cols, axis=1).astype(BF16)


def _prep_w_qb(w_qb):
    nope = w_qb[:, :, :MLA_NOPE].reshape(MLA_Q_LORA, MLA_HEADS * MLA_NOPE)
    pe = jnp.pad(w_qb[:, :, MLA_NOPE:], ((0, 0), (0, 0), (0, LANES - MLA_ROPE)))
    return jnp.concatenate([nope, pe.reshape(MLA_Q_LORA, MLA_HEADS * LANES)], axis=1).astype(BF16)


def _layer(l, x, tabs, caches, page_table, lw):
    (g_mix_pre, w_in, b_fox_f, b_gate, diff_lambda, diff_subln, g_mla_qa, w_mla_qb, g_mla_kva,
     w_mla_kvb, w_branch, w_o, g_mix_post, g_mlp_pre, w_up, w_down, g_mlp_post) = lw
    cache_diff, cache_fox, cache_logf_t, cache_mla = caches
    c128, s128, c64, s64 = tabs
    lam_init = 0.8 - 0.6 * math.exp(-0.3 * l)

    proj = norm_matmul(x, 0, D_MODEL, g_mix_pre, _prep_w_in(w_in), tm=512, tn=512, name="in_proj")
    b_fox_pad = jnp.pad(b_fox_f, (0, LANES - FOX_HEADS)).reshape(1, LANES)
    (dq, fq, new_diff, dkv_bf, fkv_bf, new_logf, logf_pad, new_mla, mla_bf) = inproj_epilogue(
        proj, tabs, g_mla_kva, b_fox_pad, tm=256)
    new_fox = proj[:, C_FKV:C_FKV + 512]

    q = norm_matmul(proj, C_MQA // MLA_Q_LORA, MLA_Q_LORA, g_mla_qa, _prep_w_qb(w_mla_qb),
                    tm=512, tn=512, name="mla_q_proj")
    w_uk = jnp.transpose(w_mla_kvb[:, :, :MLA_NOPE], (1, 2, 0)).astype(BF16)
    w_uv = jnp.transpose(w_mla_kvb[:, :, MLA_NOPE:], (1, 0, 2)).astype(BF16)
    qcat = mla_q_prep(q, w_uk, c64, s64, tm=256)

    logf_t = jnp.transpose(new_logf[:M_PROMPT].reshape(BATCH, SEQ, FOX_HEADS), (0, 2, 1))
    cum_pad, cum_t = fox_cum_prompt(logf_pad, logf_t)
    o_diff_p = flash_diff(dq, dkv_bf, diff_lambda, diff_subln, lam_init, tq=256, tk=256)
    o_fox_p = flash_fox(fq, fkv_bf, cum_pad, cum_t, tq=256, tk=256)
    o_lat_p = flash_mla(qcat, mla_bf, tq=256, tk=256)

    def new_page(rows):
        r = rows[M_PROMPT:].reshape(DEC_BATCH, DEC_SEQ * KV_PARTS, LANES)
        return jnp.pad(r, ((0, 0), (0, (PAGE_SIZE - DEC_SEQ) * KV_PARTS), (0, 0)))

    dq_s = dq[M_PROMPT:].reshape(DEC_BATCH, DEC_SEQ * DIFF_HEADS, 2, DIFF_HD)
    zq = jnp.zeros((DEC_BATCH, DEC_SEQ * DIFF_HEADS, DIFF_HD), BF16)
    qa_diff = jnp.concatenate([jnp.concatenate([dq_s[:, :, 0], zq], axis=-1),
                               jnp.concatenate([zq, dq_s[:, :, 1]], axis=-1)], axis=1)
    o_diff_s = decode_attention("diff", page_table, cache_diff, l, qa_diff, new_page(new_diff),
                                lam=diff_lambda, subln=diff_subln, lam_init=lam_init)

    fq_s = fq[M_PROMPT:].reshape(DEC_BATCH, DEC_SEQ, FOX_HEADS, FOX_HD)
    first = (jnp.arange(FOX_HEADS) < FOX_GROUP)[None, None, :, None]
    zf = jnp.zeros_like(fq_s)
    qa_fox = jnp.concatenate([jnp.where(first, fq_s, zf), jnp.where(first, zf, fq_s)], axis=-1)
    qa_fox = qa_fox.reshape(DEC_BATCH, DEC_ROWS, 2 * FOX_HD)
    nl_t = jnp.transpose(new_logf[M_PROMPT:].reshape(DEC_BATCH, DEC_SEQ, FOX_HEADS), (0, 2, 1))
    nl_t = jnp.pad(nl_t, ((0, 0), (0, 0), (0, LANES - DEC_SEQ)))
    d_past, cn = fox_cum_decode(page_table, cache_logf_t, l, nl_t)
    o_fox_s = decode_attention("fox", page_table, cache_fox, l, qa_fox, new_page(new_fox), cn=cn, d=d_past)

    qc_s = qcat[M_PROMPT:].reshape(DEC_BATCH, DEC_ROWS, MLA_QW)
    mla_new = jnp.transpose(new_mla[M_PROMPT:].reshape(DEC_BATCH, DEC_SEQ, MLA_KV_LORA + MLA_ROPE), (0, 2, 1))
    mla_new = jnp.pad(mla_new, ((0, 0), (0, 0), (0, PAGE_SIZE - DEC_SEQ)))
    o_lat_s = decode_attention("mla", page_table, cache_mla, l, qc_s[..., :MLA_KV_LORA], mla_new,
                               qb=qc_s[..., MLA_KV_LORA:MLA_KV_LORA + MLA_ROPE])

    o_diff = jnp.concatenate([o_diff_p, o_diff_s.reshape(M_SAMPLE, BRANCH_W)], axis=0)
    o_fox = jnp.concatenate([o_fox_p, o_fox_s.reshape(M_SAMPLE, BRANCH_W)], axis=0)
    o_lat = jnp.concatenate([o_lat_p, o_lat_s.reshape(M_SAMPLE, MLA_HEADS * MLA_KV_LORA)], axis=0)
    o_mla = mla_out_proj(o_lat, w_uv, tm=512)
    o_cat = jnp.concatenate([o_diff, o_fox, o_mla], axis=1)

    merged = gated_merge(o_cat, proj, b_gate, w_branch.astype(BF16), tm=512, tn=512)
    x = matmul_norm_res(merged, w_o.astype(BF16), g_mix_post, x, tm=512, tk=1024, name="out_proj")
    h = norm_matmul(x, 0, D_MODEL, g_mlp_pre, w_up.astype(BF16), tm=512, tn=1024, act="relu2",
                    out_dtype=BF16, name="mlp_up")
    x = matmul_norm_res(h, w_down.astype(BF16), g_mlp_post, x, tm=512, tk=1024, name="mlp_down")
    return x, (new_diff, new_fox, new_logf, new_mla)


def kernel(x_prompt, x_sample, cache_diff_kv, cache_fox_kv, cache_fox_logf, cache_mla_kv, page_table,
           g_mix_pre, w_in, b_fox_f, b_gate, diff_lambda, diff_subln, g_mla_qa, w_mla_qb, g_mla_kva,
           w_mla_kvb, w_branch, w_o, g_mix_post, g_mlp_pre, w_up, w_down, g_mlp_post):
    n_phys = cache_diff_kv.shape[1]
    caches = (cache_diff_kv.reshape(DEPTH, n_phys, PAGE_SIZE * KV_PARTS, LANES),
              cache_fox_kv.reshape(DEPTH, n_phys, PAGE_SIZE * KV_PARTS, LANES),
              jnp.transpose(cache_fox_logf, (0, 1, 3, 2)),
              jnp.transpose(cache_mla_kv, (0, 1, 3, 2)))
    tabs = _rope_tables()
    x = jnp.concatenate([x_prompt.reshape(M_PROMPT, D_MODEL), x_sample.reshape(M_SAMPLE, D_MODEL)], axis=0)
    states = []
    for l in range(DEPTH):
        lw = (g_mix_pre[l], w_in[l], b_fox_f[l], b_gate[l], diff_lambda[l], diff_subln[l], g_mla_qa[l],
              w_mla_qb[l], g_mla_kva[l], w_mla_kvb[l], w_branch[l], w_o[l], g_mix_post[l], g_mlp_pre[l],
              w_up[l], w_down[l], g_mlp_post[l])
        x, st = _layer(l, x, tabs, caches, page_table, lw)
        states.append(st)

    def out(i, shape_p, shape_s):
        rows = jnp.stack([s[i] for s in states], axis=0)
        return (rows[:, :M_PROMPT].reshape((DEPTH, BATCH, SEQ) + shape_p),
                rows[:, M_PROMPT:].reshape((DEPTH, DEC_BATCH, DEC_SEQ) + shape_s))

    diff_p, diff_s = out(0, (2, 1, 2 * DIFF_HD), (2, 1, 2 * DIFF_HD))
    fox_p, fox_s = out(1, (2, FOX_KV_HEADS, FOX_HD), (2, FOX_KV_HEADS, FOX_HD))
    logf_p, logf_s = out(2, (FOX_HEADS,), (FOX_HEADS,))
    mla_p, mla_s = out(3, (MLA_KV_LORA + MLA_ROPE,), (MLA_KV_LORA + MLA_ROPE,))
    y_p = x[:M_PROMPT].reshape(BATCH, SEQ, D_MODEL)
    y_s = x[M_PROMPT:].reshape(DEC_BATCH, DEC_SEQ, D_MODEL)
    return (y_p, y_s, diff_p, fox_p, logf_p, mla_p, diff_s, fox_s, logf_s, mla_s)
```

```python
import functools
import math

import numpy as np
import jax
import jax.numpy as jnp
from jax import lax
from jax.experimental import pallas as pl
from jax.experimental.pallas import tpu as pltpu

D_MODEL = 2048
BATCH = 4
SEQ = 2048
DEPTH = 2
DEC_BATCH = 128
DEC_SEQ = 4
PAST_LEN = 8192
PAGE_SIZE = 128
N_PAGES = PAST_LEN // PAGE_SIZE
BRANCH_W = D_MODEL // 2
DIFF_HD = 128
DIFF_HEADS = BRANCH_W // (2 * DIFF_HD)
FOX_HD = 128
FOX_HEADS = BRANCH_W // FOX_HD
FOX_KV_HEADS = 2
FOX_GROUP = FOX_HEADS // FOX_KV_HEADS
MLA_V = 128
MLA_HEADS = BRANCH_W // MLA_V
MLA_NOPE = 128
MLA_ROPE = 64
MLA_Q_LORA = D_MODEL // 4
MLA_KV_LORA = D_MODEL // 8
N_BRANCH = 3
D_FF = 4 * D_MODEL
ROPE_THETA = 10000.0
EPS = 1e-6

M_PROMPT = BATCH * SEQ
M_SAMPLE = DEC_BATCH * DEC_SEQ
M_ALL = M_PROMPT + M_SAMPLE
TM_WIDE = M_ALL // 8

LANES = 128
NEG = -1e30
VMEM_LIMIT = 48 * 1024 * 1024

C_DQ = 0
C_FQ = 1024
C_DKV = 2048
C_FKV = 2560
C_MQA = 3072
C_MKV = 3584
C_GATE = 4096
IN_COLS_PAD = C_GATE + N_BRANCH * D_MODEL
MLA_QW = MLA_KV_LORA + LANES
MLA_KW = MLA_KV_LORA + LANES

BF16 = jnp.bfloat16
F32 = jnp.float32


def _cparams(sem):
    return pltpu.CompilerParams(dimension_semantics=sem, vmem_limit_bytes=VMEM_LIMIT)


def _dot(a, b):
    return jnp.dot(a, b, preferred_element_type=F32)


def _dot_nt(a, b):
    return lax.dot_general(a, b, (((1,), (1,)), ((), ())), preferred_element_type=F32)


def _dot_exact(a, b):
    return jnp.dot(a, b, preferred_element_type=F32, precision=lax.Precision.HIGHEST)


def _rms(x, g):
    ms = jnp.mean(x * x, axis=-1, keepdims=True)
    return x * lax.rsqrt(ms + EPS) * g


def _norm_matmul_kernel(x_ref, g_ref, w_ref, o_ref, u_ref, *, act):
    @pl.when(pl.program_id(1) == 0)
    def _():
        u_ref[...] = _rms(x_ref[...].astype(F32), g_ref[...]).astype(BF16)

    acc = _dot(u_ref[...], w_ref[...])
    if act == "relu2":
        acc = jnp.square(jnp.maximum(acc, 0.0))
    o_ref[...] = acc.astype(o_ref.dtype)


def norm_matmul(x, x_col, k, g, w, *, tm, tn, act=None, out_dtype=F32, name):
    m = x.shape[0]
    n = w.shape[1]
    return pl.pallas_call(
        functools.partial(_norm_matmul_kernel, act=act),
        out_shape=jax.ShapeDtypeStruct((m, n), out_dtype),
        grid=(m // tm, n // tn),
        in_specs=[
            pl.BlockSpec((tm, k), lambda i, j: (i, x_col)),
            pl.BlockSpec((1, k), lambda i, j: (0, 0)),
            pl.BlockSpec((k, tn), lambda i, j: (0, j)),
        ],
        out_specs=pl.BlockSpec((tm, tn), lambda i, j: (i, j)),
        scratch_shapes=[pltpu.VMEM((tm, k), BF16)],
        compiler_params=_cparams(("parallel", "arbitrary")),
        name=name,
    )(x, g.reshape(1, k), w)


def _matmul_norm_res_kernel(a_ref, w_ref, g_ref, r_ref, o_ref, acc_ref, *, nk):
    k = pl.program_id(1)

    @pl.when(k == 0)
    def _():
        acc_ref[...] = jnp.zeros_like(acc_ref)

    acc_ref[...] += _dot(a_ref[...], w_ref[...])

    @pl.when(k == nk - 1)
    def _():
        o_ref[...] = r_ref[...] + _rms(acc_ref[...], g_ref[...])


def matmul_norm_res(a, w, g, resid, *, tm, tk, name):
    m, kdim = a.shape
    n = w.shape[1]
    nk = kdim // tk
    return pl.pallas_call(
        functools.partial(_matmul_norm_res_kernel, nk=nk),
        out_shape=jax.ShapeDtypeStruct((m, n), F32),
        grid=(m // tm, nk),
        in_specs=[
            pl.BlockSpec((tm, tk), lambda i, k: (i, k)),
            pl.BlockSpec((tk, n), lambda i, k: (k, 0)),
            pl.BlockSpec((1, n), lambda i, k: (0, 0)),
            pl.BlockSpec((tm, n), lambda i, k: (i, 0)),
        ],
        out_specs=pl.BlockSpec((tm, n), lambda i, k: (i, 0)),
        scratch_shapes=[pltpu.VMEM((tm, n), F32)],
        compiler_params=_cparams(("parallel", "arbitrary")),
        name=name,
    )(a, w, g.reshape(1, n), resid)


def _merge_kernel(o_ref, g0_ref, g1_ref, g2_ref, b_ref, wb_ref, m_ref):
    acc = None
    for n, g_ref in enumerate((g0_ref, g1_ref, g2_ref)):
        y = _dot(o_ref[:, n * BRANCH_W:(n + 1) * BRANCH_W], wb_ref[n])
        term = jax.nn.sigmoid(g_ref[...] + b_ref[n]) * y
        acc = term if acc is None else acc + term
    m_ref[...] = acc.astype(m_ref.dtype)


def gated_merge(o_cat, proj, b_gate, w_branch, *, tm, tn):
    m = o_cat.shape[0]
    nj = D_MODEL // tn
    gate_spec = lambda n: pl.BlockSpec(
        (tm, tn), lambda i, j, n=n: (i, (C_GATE + n * D_MODEL) // tn + j))
    return pl.pallas_call(
        _merge_kernel,
        out_shape=jax.ShapeDtypeStruct((m, D_MODEL), BF16),
        grid=(m // tm, nj),
        in_specs=[
            pl.BlockSpec((tm, N_BRANCH * BRANCH_W), lambda i, j: (i, 0)),
            gate_spec(0), gate_spec(1), gate_spec(2),
            pl.BlockSpec((N_BRANCH, 1, tn), lambda i, j: (0, 0, j)),
            pl.BlockSpec((N_BRANCH, BRANCH_W, tn), lambda i, j: (0, 0, j)),
        ],
        out_specs=pl.BlockSpec((tm, tn), lambda i, j: (i, j)),
        compiler_params=_cparams(("parallel", "arbitrary")),
        name="gated_merge",
    )(o_cat, proj, proj, proj, b_gate.reshape(N_BRANCH, 1, D_MODEL), w_branch)


def _rope128(x, cos_f, sin_s):
    return x * cos_f + pltpu.roll(x, DIFF_HD // 2, 1) * sin_s


def _rope64(x, cos_f, sin_s):
    lane = lax.broadcasted_iota(jnp.int32, x.shape, 1)
    half = MLA_ROPE // 2
    rot = jnp.where((lane % MLA_ROPE) < half,
                    pltpu.roll(x, LANES - half, 1), pltpu.roll(x, half, 1))
    return x * cos_f + rot * sin_s


def _log_sigmoid(x):
    return jnp.minimum(x, 0.0) - jnp.log1p(jnp.exp(-jnp.abs(x)))


def _inproj_epilogue_kernel(dq_ref, fq_ref, dkv_ref, fkv_ref, mkv_ref, c128_ref, s128_ref,
                            c64_ref, s64_ref, gkv_ref, bf_ref,
                            dq_o, fq_o, dkv_o, dkv_bf_o, fkv_bf_o, logf_o, logfp_o, mla_o, mla_bf_o):
    cos_f, sin_s = c128_ref[...], s128_ref[...]
    qscale = DIFF_HD ** -0.5
    for c in range(2 * DIFF_HEADS):
        sl = slice(c * DIFF_HD, (c + 1) * DIFF_HD)
        dq_o[:, sl] = (_rope128(dq_ref[:, sl], cos_f, sin_s) * qscale).astype(BF16)
    fq_o[...] = (fq_ref[...] * (FOX_HD ** -0.5)).astype(BF16)
    for c in range(2):
        sl = slice(c * DIFF_HD, (c + 1) * DIFF_HD)
        kr = _rope128(dkv_ref[:, sl], cos_f, sin_s)
        dkv_o[:, sl] = kr
        dkv_bf_o[:, sl] = kr.astype(BF16)
    v = dkv_ref[:, 2 * DIFF_HD:]
    dkv_o[:, 2 * DIFF_HD:] = v
    dkv_bf_o[:, 2 * DIFF_HD:] = v.astype(BF16)
    fkv_bf_o[...] = fkv_ref[...].astype(BF16)
    lf = _log_sigmoid(mkv_ref[:, MLA_KV_LORA + LANES:] + bf_ref[...])
    logfp_o[...] = lf
    logf_o[...] = lf[:, :FOX_HEADS]
    ckv = _rms(mkv_ref[:, :MLA_KV_LORA], gkv_ref[...])
    kpe = _rope64(mkv_ref[:, MLA_KV_LORA:MLA_KV_LORA + LANES], c64_ref[...], s64_ref[...])
    mla_o[:, :MLA_KV_LORA] = ckv
    mla_o[:, MLA_KV_LORA:] = kpe[:, :MLA_ROPE]
    mla_bf_o[:, :MLA_KV_LORA] = ckv.astype(BF16)
    mla_bf_o[:, MLA_KV_LORA:] = kpe.astype(BF16)


def inproj_epilogue(proj, tabs, g_kva, b_fox_pad, *, tm):
    m = proj.shape[0]
    c128, s128, c64, s64 = tabs
    row = lambda w, c: pl.BlockSpec((tm, w), lambda i, c=c: (i, c))
    full = lambda w: pl.BlockSpec((1, w), lambda i: (0, 0))
    outs = [
        ((m, 1024), BF16), ((m, 1024), BF16), ((m, 512), F32), ((m, 512), BF16), ((m, 512), BF16),
        ((m, FOX_HEADS), F32), ((m, LANES), F32), ((m, MLA_KV_LORA + MLA_ROPE), F32), ((m, MLA_KW), BF16),
    ]
    return pl.pallas_call(
        _inproj_epilogue_kernel,
        out_shape=[jax.ShapeDtypeStruct(s, d) for s, d in outs],
        grid=(m // tm,),
        in_specs=[
            row(1024, C_DQ // 1024), row(1024, C_FQ // 1024), row(512, C_DKV // 512),
            row(512, C_FKV // 512), row(512, C_MKV // 512),
            row(LANES, 0), row(LANES, 0), row(LANES, 0), row(LANES, 0),
            full(MLA_KV_LORA), full(LANES),
        ],
        out_specs=[pl.BlockSpec((tm, s[1]), lambda i: (i, 0)) for s, _ in outs],
        compiler_params=_cparams(("parallel",)),
        name="inproj_epilogue",
    )(proj, proj, proj, proj, proj, c128, s128, c64, s64, g_kva.reshape(1, -1), b_fox_pad)


def _mla_q_kernel(q_ref, wuk_ref, c64_ref, s64_ref, o_ref):
    scale = (MLA_NOPE + MLA_ROPE) ** -0.5
    for h in range(MLA_HEADS):
        qn = q_ref[:, h * MLA_NOPE:(h + 1) * MLA_NOPE].astype(BF16)
        qlat = _dot(qn, wuk_ref[h])
        base = MLA_HEADS * MLA_NOPE + h * LANES
        qpe = _rope64(q_ref[:, base:base + LANES], c64_ref[...], s64_ref[...])
        o_ref[:, h * MLA_QW:h * MLA_QW + MLA_KV_LORA] = (qlat * scale).astype(BF16)
        o_ref[:, h * MLA_QW + MLA_KV_LORA:(h + 1) * MLA_QW] = (qpe * scale).astype(BF16)


def mla_q_prep(q, w_uk, c64, s64, *, tm):
    m, qw = q.shape
    return pl.pallas_call(
        _mla_q_kernel,
        out_shape=jax.ShapeDtypeStruct((m, MLA_HEADS * MLA_QW), BF16),
        grid=(m // tm,),
        in_specs=[
            pl.BlockSpec((tm, qw), lambda i: (i, 0)),
            pl.BlockSpec((MLA_HEADS, MLA_NOPE, MLA_KV_LORA), lambda i: (0, 0, 0)),
            pl.BlockSpec((tm, LANES), lambda i: (i, 0)),
            pl.BlockSpec((tm, LANES), lambda i: (i, 0)),
        ],
        out_specs=pl.BlockSpec((tm, MLA_HEADS * MLA_QW), lambda i: (i, 0)),
        compiler_params=_cparams(("parallel",)),
        name="mla_q_prep",
    )(q, w_uk, c64, s64)


def _mla_out_kernel(o_ref, wuv_ref, y_ref):
    for h in range(MLA_HEADS):
        y = _dot(o_ref[:, h * MLA_KV_LORA:(h + 1) * MLA_KV_LORA], wuv_ref[h])
        y_ref[:, h * MLA_V:(h + 1) * MLA_V] = y.astype(BF16)


def mla_out_proj(o_lat, w_uv, *, tm):
    m = o_lat.shape[0]
    return pl.pallas_call(
        _mla_out_kernel,
        out_shape=jax.ShapeDtypeStruct((m, BRANCH_W), BF16),
        grid=(m // tm,),
        in_specs=[
            pl.BlockSpec((tm, MLA_HEADS * MLA_KV_LORA), lambda i: (i, 0)),
            pl.BlockSpec((MLA_HEADS, MLA_KV_LORA, MLA_V), lambda i: (0, 0, 0)),
        ],
        out_specs=pl.BlockSpec((tm, BRANCH_W), lambda i: (i, 0)),
        compiler_params=_cparams(("parallel",)),
        name="mla_out_proj",
    )(o_lat, w_uv)


CUM_BLK = 256


def _fox_cum_prompt_kernel(x_ref, xt_ref, cum_ref, cumt_ref):
    r = lax.broadcasted_iota(jnp.int32, (CUM_BLK, CUM_BLK), 0)
    c = lax.broadcasted_iota(jnp.int32, (CUM_BLK, CUM_BLK), 1)
    lower = (c <= r).astype(F32)
    upper = (r <= c).astype(F32)
    carry = jnp.zeros((1, LANES), F32)
    carry_t = jnp.zeros((FOX_HEADS, 1), F32)
    for b in range(SEQ // CUM_BLK):
        sl = slice(b * CUM_BLK, (b + 1) * CUM_BLK)
        cb = _dot_exact(lower, x_ref[sl, :]) + carry
        cum_ref[sl, :] = cb
        carry = cb[CUM_BLK - 1:CUM_BLK, :]
        cbt = _dot_exact(xt_ref[0, :, sl], upper) + carry_t
        cumt_ref[0, :, sl] = cbt
        carry_t = cbt[:, CUM_BLK - 1:CUM_BLK]


def fox_cum_prompt(logf_pad, logf_t):
    return pl.pallas_call(
        _fox_cum_prompt_kernel,
        out_shape=[jax.ShapeDtypeStruct((M_PROMPT, LANES), F32),
                   jax.ShapeDtypeStruct((BATCH, FOX_HEADS, SEQ), F32)],
        grid=(BATCH,),
        in_specs=[pl.BlockSpec((SEQ, LANES), lambda b: (b, 0)),
                  pl.BlockSpec((1, FOX_HEADS, SEQ), lambda b: (b, 0, 0))],
        out_specs=[pl.BlockSpec((SEQ, LANES), lambda b: (b, 0)),
                   pl.BlockSpec((1, FOX_HEADS, SEQ), lambda b: (b, 0, 0))],
        compiler_params=_cparams(("parallel",)),
        name="fox_cum_prompt",
    )(logf_pad, logf_t)


def _softmax_init(m_ref, l_ref, acc_ref):
    m_ref[...] = jnp.full_like(m_ref, NEG)
    l_ref[...] = jnp.zeros_like(l_ref)
    acc_ref[...] = jnp.zeros_like(acc_ref)


def _softmax_step(s, pv_fn, m_ref, l_ref, acc_ref, idx):
    m_prev = m_ref[idx]
    m_new = jnp.maximum(m_prev, jnp.max(s, axis=-1, keepdims=True))
    alpha = jnp.exp(m_prev - m_new)
    p = jnp.exp(s - m_new)
    l_ref[idx] = alpha * l_ref[idx] + jnp.sum(p, axis=-1, keepdims=True)
    acc_ref[idx] = alpha * acc_ref[idx] + pv_fn(p.astype(BF16))
    m_ref[idx] = m_new


def _causal_mask(qi, ki, tq, tk, nstack):
    qpos = qi * tq + lax.broadcasted_iota(jnp.int32, (nstack * tq, tk), 0) % tq
    kpos = ki * tk + lax.broadcasted_iota(jnp.int32, (nstack * tq, tk), 1)
    return kpos <= qpos


def _diff_lambda(lam_ref, lam_init):
    lf = lam_ref[...]
    a = jnp.sum(lf[0:1] * lf[1:2], axis=-1, keepdims=True)
    b = jnp.sum(lf[2:3] * lf[3:4], axis=-1, keepdims=True)
    return jnp.exp(a) - jnp.exp(b) + lam_init


def _diff_finish(o1, o2, lam, subln, lam_init):
    return _rms(o1 - lam * o2, subln) * (1.0 - lam_init)


def _flash_diff_kernel(q_ref, kv_ref, lam_ref, subln_ref, o_ref, m_ref, l_ref, acc_ref,
                       *, tq, tk, nk, lam_init):
    qi, ki = pl.program_id(1), pl.program_id(2)

    @pl.when(ki == 0)
    def _():
        _softmax_init(m_ref, l_ref, acc_ref)

    @pl.when(ki * tk <= qi * tq + tq - 1)
    def _():
        mask = _causal_mask(qi, ki, tq, tk, 1)
        v = kv_ref[:, 2 * DIFF_HD:]
        for c in range(2 * DIFF_HEADS):
            mp = c % 2
            s = _dot_nt(q_ref[:, c * DIFF_HD:(c + 1) * DIFF_HD], kv_ref[:, mp * DIFF_HD:(mp + 1) * DIFF_HD])
            s = jnp.where(mask, s, NEG)
            _softmax_step(s, lambda p: _dot(p, v), m_ref, l_ref, acc_ref, c)

    @pl.when(ki == nk - 1)
    def _():
        lam = _diff_lambda(lam_ref, lam_init)
        for h in range(DIFF_HEADS):
            o1 = acc_ref[2 * h] / l_ref[2 * h]
            o2 = acc_ref[2 * h + 1] / l_ref[2 * h + 1]
            o = _diff_finish(o1, o2, lam, subln_ref[...], lam_init)
            o_ref[:, h * 2 * DIFF_HD:(h + 1) * 2 * DIFF_HD] = o.astype(BF16)


def _kv_clamp(qi, ki, tq, tk):
    return jnp.minimum(ki, (qi * tq + tq - 1) // tk)


def flash_diff(dq, dkv_bf, diff_lambda, subln, lam_init, *, tq, tk):
    nq, nk = SEQ // tq, SEQ // tk
    return pl.pallas_call(
        functools.partial(_flash_diff_kernel, tq=tq, tk=tk, nk=nk, lam_init=lam_init),
        out_shape=jax.ShapeDtypeStruct((M_PROMPT, BRANCH_W), BF16),
        grid=(BATCH, nq, nk),
        in_specs=[
            pl.BlockSpec((tq, BRANCH_W), lambda b, qi, ki: (b * nq + qi, 0)),
            pl.BlockSpec((tk, 512), lambda b, qi, ki: (b * nk + _kv_clamp(qi, ki, tq, tk), 0)),
            pl.BlockSpec((4, DIFF_HD), lambda b, qi, ki: (0, 0)),
            pl.BlockSpec((1, 2 * DIFF_HD), lambda b, qi, ki: (0, 0)),
        ],
        out_specs=pl.BlockSpec((tq, BRANCH_W), lambda b, qi, ki: (b * nq + qi, 0)),
        scratch_shapes=[pltpu.VMEM((2 * DIFF_HEADS, tq, 1), F32), pltpu.VMEM((2 * DIFF_HEADS, tq, 1), F32),
                        pltpu.VMEM((2 * DIFF_HEADS, tq, 2 * DIFF_HD), F32)],
        compiler_params=_cparams(("parallel", "parallel", "arbitrary")),
        name="flash_diff",
    )(dq, dkv_bf, diff_lambda, subln.reshape(1, -1))


def _flash_fox_kernel(q_ref, k_ref, v_ref, cq_ref, ck_ref, o_ref, m_ref, l_ref, acc_ref, *, tq, tk, nk):
    g, qi, ki = pl.program_id(1), pl.program_id(2), pl.program_id(3)

    @pl.when(ki == 0)
    def _():
        _softmax_init(m_ref, l_ref, acc_ref)

    @pl.when(ki * tk <= qi * tq + tq - 1)
    def _():
        mask = _causal_mask(qi, ki, tq, tk, FOX_GROUP)
        k, v = k_ref[...], v_ref[...]
        lane = lax.broadcasted_iota(jnp.int32, (tq, LANES), 1)
        sub = lax.broadcasted_iota(jnp.int32, (FOX_HEADS, tk), 0)
        bias = []
        for hl in range(FOX_GROUP):
            h = g * FOX_GROUP + hl
            cq = jnp.sum(jnp.where(lane == h, cq_ref[...], 0.0), axis=-1, keepdims=True)
            ck = jnp.sum(jnp.where(sub == h, ck_ref[0], 0.0), axis=0, keepdims=True)
            bias.append(cq - ck)
        q = jnp.concatenate([q_ref[:, hl * FOX_HD:(hl + 1) * FOX_HD] for hl in range(FOX_GROUP)], axis=0)
        s = _dot_nt(q, k) + jnp.concatenate(bias, axis=0)
        s = jnp.where(mask, s, NEG)
        _softmax_step(s, lambda p: _dot(p, v), m_ref, l_ref, acc_ref, 0)

    @pl.when(ki == nk - 1)
    def _():
        for hl in range(FOX_GROUP):
            rows = slice(hl * tq, (hl + 1) * tq)
            o_ref[:, hl * FOX_HD:(hl + 1) * FOX_HD] = (acc_ref[0, rows] / l_ref[0, rows]).astype(BF16)


def flash_fox(fq, fkv_bf, cum_pad, cum_t, *, tq, tk):
    nq, nk = SEQ // tq, SEQ // tk
    gw = FOX_GROUP * FOX_HD
    return pl.pallas_call(
        functools.partial(_flash_fox_kernel, tq=tq, tk=tk, nk=nk),
        out_shape=jax.ShapeDtypeStruct((M_PROMPT, BRANCH_W), BF16),
        grid=(BATCH, FOX_KV_HEADS, nq, nk),
        in_specs=[
            pl.BlockSpec((tq, gw), lambda b, g, qi, ki: (b * nq + qi, g)),
            pl.BlockSpec((tk, FOX_HD), lambda b, g, qi, ki: (b * nk + _kv_clamp(qi, ki, tq, tk), g)),
            pl.BlockSpec((tk, FOX_HD), lambda b, g, qi, ki: (b * nk + _kv_clamp(qi, ki, tq, tk), FOX_KV_HEADS + g)),
            pl.BlockSpec((tq, LANES), lambda b, g, qi, ki: (b * nq + qi, 0)),
            pl.BlockSpec((1, FOX_HEADS, tk), lambda b, g, qi, ki: (b, 0, _kv_clamp(qi, ki, tq, tk))),
        ],
        out_specs=pl.BlockSpec((tq, gw), lambda b, g, qi, ki: (b * nq + qi, g)),
        scratch_shapes=[pltpu.VMEM((1, FOX_GROUP * tq, 1), F32), pltpu.VMEM((1, FOX_GROUP * tq, 1), F32),
                        pltpu.VMEM((1, FOX_GROUP * tq, FOX_HD), F32)],
        compiler_params=_cparams(("parallel", "parallel", "parallel", "arbitrary")),
        name="flash_fox",
    )(fq, fkv_bf, fkv_bf, cum_pad, cum_t)


def _flash_mla_kernel(q_ref, kv_ref, o_ref, m_ref, l_ref, acc_ref, *, tq, tk, nk):
    qi, ki = pl.program_id(1), pl.program_id(2)

    @pl.when(ki == 0)
    def _():
        _softmax_init(m_ref, l_ref, acc_ref)

    @pl.when(ki * tk <= qi * tq + tq - 1)
    def _():
        mask = _causal_mask(qi, ki, tq, tk, 1)
        kv = kv_ref[...]
        v = kv_ref[:, :MLA_KV_LORA]
        for h in range(MLA_HEADS):
            s = _dot_nt(q_ref[:, h * MLA_QW:(h + 1) * MLA_QW], kv)
            s = jnp.where(mask, s, NEG)
            _softmax_step(s, lambda p: _dot(p, v), m_ref, l_ref, acc_ref, h)

    @pl.when(ki == nk - 1)
    def _():
        for h in range(MLA_HEADS):
            o_ref[:, h * MLA_KV_LORA:(h + 1) * MLA_KV_LORA] = (acc_ref[h] / l_ref[h]).astype(BF16)


def flash_mla(qcat, mla_bf, *, tq, tk):
    nq, nk = SEQ // tq, SEQ // tk
    return pl.pallas_call(
        functools.partial(_flash_mla_kernel, tq=tq, tk=tk, nk=nk),
        out_shape=jax.ShapeDtypeStruct((M_PROMPT, MLA_HEADS * MLA_KV_LORA), BF16),
        grid=(BATCH, nq, nk),
        in_specs=[
            pl.BlockSpec((tq, MLA_HEADS * MLA_QW), lambda b, qi, ki: (b * nq + qi, 0)),
            pl.BlockSpec((tk, MLA_KW), lambda b, qi, ki: (b * nk + _kv_clamp(qi, ki, tq, tk), 0)),
        ],
        out_specs=pl.BlockSpec((tq, MLA_HEADS * MLA_KV_LORA), lambda b, qi, ki: (b * nq + qi, 0)),
        scratch_shapes=[pltpu.VMEM((MLA_HEADS, tq, 1), F32), pltpu.VMEM((MLA_HEADS, tq, 1), F32),
                        pltpu.VMEM((MLA_HEADS, tq, MLA_KV_LORA), F32)],
        compiler_params=_cparams(("parallel", "parallel", "arbitrary")),
        name="flash_mla",
    )(qcat, mla_bf)


def _split3(x):
    hi = x.astype(BF16)
    r = x - hi.astype(F32)
    mid = r.astype(BF16)
    lo = (r - mid.astype(F32)).astype(BF16)
    return hi, mid, lo


def _fox_cum_decode_kernel(pt_ref, nl_ref, *rest):
    page_refs = rest[:N_PAGES]
    d_ref, cn_ref = rest[N_PAGES:]
    rows = N_PAGES * FOX_HEADS
    j = lax.broadcasted_iota(jnp.int32, (LANES, LANES), 0)
    s = lax.broadcasted_iota(jnp.int32, (LANES, LANES), 1)
    incl = (j <= s).astype(F32).astype(BF16)
    strict = (j > s).astype(F32).astype(BF16)
    hi, mid, lo = _split3(nl_ref[0])
    cn_ref[0] = _dot(hi, incl) + _dot(mid, incl) + _dot(lo, incl)

    x = jnp.concatenate([p[0, 0] for p in page_refs], axis=0)
    hi, mid, lo = _split3(x)
    d_in = _dot(hi, strict) + _dot(mid, strict) + _dot(lo, strict)
    tot = jnp.broadcast_to(jnp.sum(x, axis=-1, keepdims=True), (rows, LANES))
    later = jnp.zeros((FOX_HEADS, LANES), F32)
    for p in range(N_PAGES - 1, -1, -1):
        sl = slice(p * FOX_HEADS, (p + 1) * FOX_HEADS)
        d_ref[0, p] = d_in[sl] + later
        later = later + tot[sl]


def fox_cum_decode(page_table, logf_t_cache, layer, new_logf_t):
    page_spec = lambda p: pl.BlockSpec(
        (1, 1, FOX_HEADS, LANES), lambda b, pt, p=p: (layer, pt[b, p], 0, 0))
    grid_spec = pltpu.PrefetchScalarGridSpec(
        num_scalar_prefetch=1,
        grid=(DEC_BATCH,),
        in_specs=[pl.BlockSpec((1, FOX_HEADS, LANES), lambda b, pt: (b, 0, 0))]
        + [page_spec(p) for p in range(N_PAGES)],
        out_specs=[pl.BlockSpec((1, N_PAGES, FOX_HEADS, LANES), lambda b, pt: (b, 0, 0, 0)),
                   pl.BlockSpec((1, FOX_HEADS, LANES), lambda b, pt: (b, 0, 0))],
    )
    return pl.pallas_call(
        _fox_cum_decode_kernel,
        out_shape=[jax.ShapeDtypeStruct((DEC_BATCH, N_PAGES, FOX_HEADS, LANES), F32),
                   jax.ShapeDtypeStruct((DEC_BATCH, FOX_HEADS, LANES), F32)],
        grid_spec=grid_spec,
        compiler_params=_cparams(("parallel",)),
        name="fox_cum_decode",
    )(page_table, new_logf_t, *([logf_t_cache] * N_PAGES))


DEC_ROWS = 32
DEC_KEYS = (N_PAGES + 1) * PAGE_SIZE
KV_PARTS = 4


def _decode_kernel(pt_ref, *refs, variant, lam_init):
    it = iter(refs)
    qa_ref = next(it)
    qb_ref = next(it) if variant == "mla" else None
    new_ref = next(it)
    if variant == "fox":
        cn_ref, d_ref = next(it), next(it)
    if variant == "diff":
        lam_ref, subln_ref = next(it), next(it)
    page_refs = [next(it) for _ in range(N_PAGES)]
    o_ref, s_ref, p_ref = next(it), next(it), next(it)
    qa = qa_ref[0]

    def rd(p, idx):
        return page_refs[p][(0, 0) + idx] if p < N_PAGES else new_ref[(0,) + idx]

    def part(p, j):
        return rd(p, (pl.ds(j, PAGE_SIZE, stride=KV_PARTS), slice(None)))

    def scores(p):
        if variant == "mla":
            kt = rd(p, (slice(0, MLA_KV_LORA), slice(None))).astype(BF16)
            pet = rd(p, (slice(MLA_KV_LORA, MLA_KV_LORA + MLA_ROPE), slice(None))).astype(BF16)
            return _dot(qa, kt) + _dot(qb_ref[0], pet)
        k = jnp.concatenate([part(p, 0), part(p, 1)], axis=1).astype(BF16)
        return _dot_nt(qa, k)

    def weighted_values(pm, p):
        if variant == "mla":
            return _dot_nt(pm, rd(p, (slice(0, MLA_KV_LORA), slice(None))).astype(BF16))
        v = jnp.concatenate([part(p, 2), part(p, 3)], axis=1).astype(BF16)
        return _dot(pm, v)

    if variant == "fox":
        cn = cn_ref[0]
        cn_t = [jnp.broadcast_to(cn[:, t:t + 1], (FOX_HEADS, LANES)) for t in range(DEC_SEQ)]

    for p in range(N_PAGES + 1):
        s = scores(p)
        if variant == "fox":
            d_page = d_ref[0, p] if p < N_PAGES else -cn
            s = s + jnp.concatenate([d_page + c for c in cn_t], axis=0)
        if p == N_PAGES:
            row = lax.broadcasted_iota(jnp.int32, (DEC_ROWS, PAGE_SIZE), 0)
            key = lax.broadcasted_iota(jnp.int32, (DEC_ROWS, PAGE_SIZE), 1)
            tok = (row % (DEC_SEQ * DIFF_HEADS)) // DIFF_HEADS if variant == "diff" else row // FOX_HEADS
            s = jnp.where(key <= tok, s, NEG)
        s_ref[:, p * PAGE_SIZE:(p + 1) * PAGE_SIZE] = s

    s = s_ref[...]
    e = jnp.exp(s - jnp.max(s, axis=-1, keepdims=True))
    l = jnp.sum(e, axis=-1, keepdims=True)
    p_ref[...] = e.astype(BF16)

    acc = None
    for p in range(N_PAGES + 1):
        t = weighted_values(p_ref[:, p * PAGE_SIZE:(p + 1) * PAGE_SIZE], p)
        acc = t if acc is None else acc + t

    o = acc / l
    if variant == "diff":
        half = DEC_ROWS // 2
        lam = _diff_lambda(lam_ref, lam_init)
        o = _diff_finish(o[:half], o[half:], lam, subln_ref[...], lam_init)
    elif variant == "fox":
        row = lax.broadcasted_iota(jnp.int32, (DEC_ROWS, FOX_HD), 0)
        o = jnp.where(row % FOX_HEADS < FOX_GROUP, o[:, :FOX_HD], o[:, FOX_HD:])
    o_ref[0] = o.astype(BF16)


def decode_attention(variant, page_table, cache, layer, qa, new_page, *, qb=None, cn=None, d=None,
                     lam=None, subln=None, lam_init=0.0):
    page_shape = tuple(cache.shape[2:])
    per_seq = lambda shape: pl.BlockSpec((1,) + shape, lambda b, pt: (b,) + (0,) * len(shape))
    const = lambda shape: pl.BlockSpec(shape, lambda b, pt: (0,) * len(shape))
    in_specs = [per_seq((DEC_ROWS, 2 * LANES))]
    args = [qa]
    if variant == "mla":
        in_specs.append(per_seq((DEC_ROWS, MLA_ROPE)))
        args.append(qb)
    in_specs.append(per_seq(page_shape))
    args.append(new_page)
    if variant == "fox":
        in_specs += [per_seq((FOX_HEADS, LANES)), per_seq((N_PAGES, FOX_HEADS, LANES))]
        args += [cn, d]
    if variant == "diff":
        in_specs += [const((4, DIFF_HD)), const((1, 2 * DIFF_HD))]
        args += [lam, subln.reshape(1, -1)]
    for p in range(N_PAGES):
        in_specs.append(pl.BlockSpec((1, 1) + page_shape, lambda b, pt, p=p: (layer, pt[b, p], 0, 0)))
        args.append(cache)
    out_rows = DEC_ROWS // 2 if variant == "diff" else DEC_ROWS
    out_w = FOX_HD if variant == "fox" else 2 * LANES
    grid_spec = pltpu.PrefetchScalarGridSpec(
        num_scalar_prefetch=1,
        grid=(DEC_BATCH,),
        in_specs=in_specs,
        out_specs=pl.BlockSpec((1, out_rows, out_w), lambda b, pt: (b, 0, 0)),
        scratch_shapes=[pltpu.VMEM((DEC_ROWS, DEC_KEYS), F32), pltpu.VMEM((DEC_ROWS, DEC_KEYS), BF16)],
    )
    return pl.pallas_call(
        functools.partial(_decode_kernel, variant=variant, lam_init=lam_init),
        out_shape=jax.ShapeDtypeStruct((DEC_BATCH, out_rows, out_w), BF16),
        grid_spec=grid_spec,
        compiler_params=_cparams(("parallel",)),
        name="decode_" + variant,
    )(page_table, *args)


def _rope_tables():
    pos = jnp.concatenate([jnp.tile(jnp.arange(SEQ, dtype=jnp.int32), BATCH),
                           jnp.tile(PAST_LEN + jnp.arange(DEC_SEQ, dtype=jnp.int32), DEC_BATCH)])

    def tab(dim):
        half = dim // 2
        inv_freq = 1.0 / (ROPE_THETA ** (jnp.arange(half, dtype=F32) / half))
        ang = pos.astype(F32)[:, None] * inv_freq[None, :]
        cos, sin = jnp.cos(ang), jnp.sin(ang)
        reps = LANES // dim
        return (jnp.tile(jnp.concatenate([cos, cos], axis=1), (1, reps)),
                jnp.tile(jnp.concatenate([-sin, sin], axis=1), (1, reps)))

    return tab(DIFF_HD) + tab(MLA_ROPE)


def _prep_w_in(w_in):
    sizes = (1024, 256, 256, 1024, 256, 256, FOX_HEADS, MLA_Q_LORA, MLA_KV_LORA + MLA_ROPE, N_BRANCH * D_MODEL)
    offs = np.concatenate([[0], np.cumsum(sizes)])
    dq, dk, dv, fq, fk, fv, ff, mqa, mkva, gate = [w_in[:, offs[i]:offs[i + 1]] for i in range(len(sizes))]
    z = lambda n: jnp.zeros((D_MODEL, n), w_in.dtype)
    cols = [dq, fq, dk, dv, fk, fv, mqa, mkva[:, :MLA_KV_LORA], mkva[:, MLA_KV_LORA:], z(LANES - MLA_ROPE),
            ff, z(LANES - FOX_HEADS), gate]
    return jnp.concatenate(cols, axis=1).astype(BF16)


def _prep_w_qb(w_qb):
    nope = w_qb[:, :, :MLA_NOPE].reshape(MLA_Q_LORA, MLA_HEADS * MLA_NOPE)
    pe = jnp.pad(w_qb[:, :, MLA_NOPE:], ((0, 0), (0, 0), (0, LANES - MLA_ROPE)))
    return jnp.concatenate([nope, pe.reshape(MLA_Q_LORA, MLA_HEADS * LANES)], axis=1).astype(BF16)


def _layer(l, x, tabs, caches, page_table, lw):
    (g_mix_pre, w_in, b_fox_f, b_gate, diff_lambda, diff_subln, g_mla_qa, w_mla_qb, g_mla_kva,
     w_mla_kvb, w_branch, w_o, g_mix_post, g_mlp_pre, w_up, w_down, g_mlp_post) = lw
    cache_diff, cache_fox, cache_logf_t, cache_mla = caches
    c128, s128, c64, s64 = tabs
    lam_init = 0.8 - 0.6 * math.exp(-0.3 * l)

    proj = norm_matmul(x, 0, D_MODEL, g_mix_pre, _prep_w_in(w_in), tm=TM_WIDE, tn=1024, name="in_proj")
    b_fox_pad = jnp.pad(b_fox_f, (0, LANES - FOX_HEADS)).reshape(1, LANES)
    (dq, fq, new_diff, dkv_bf, fkv_bf, new_logf, logf_pad, new_mla, mla_bf) = inproj_epilogue(
        proj, tabs, g_mla_kva, b_fox_pad, tm=256)
    new_fox = proj[:, C_FKV:C_FKV + 512]

    q = norm_matmul(proj, C_MQA // MLA_Q_LORA, MLA_Q_LORA, g_mla_qa, _prep_w_qb(w_mla_qb),
                    tm=512, tn=512, name="mla_q_proj")
    w_uk = jnp.transpose(w_mla_kvb[:, :, :MLA_NOPE], (1, 2, 0)).astype(BF16)
    w_uv = jnp.transpose(w_mla_kvb[:, :, MLA_NOPE:], (1, 0, 2)).astype(BF16)
    qcat = mla_q_prep(q, w_uk, c64, s64, tm=256)

    logf_t = jnp.transpose(new_logf[:M_PROMPT].reshape(BATCH, SEQ, FOX_HEADS), (0, 2, 1))
    cum_pad, cum_t = fox_cum_prompt(logf_pad, logf_t)
    o_diff_p = flash_diff(dq, dkv_bf, diff_lambda, diff_subln, lam_init, tq=256, tk=1024)
    o_fox_p = flash_fox(fq, fkv_bf, cum_pad, cum_t, tq=256, tk=1024)
    o_lat_p = flash_mla(qcat, mla_bf, tq=256, tk=1024)

    def new_page(rows):
        r = rows[M_PROMPT:].reshape(DEC_BATCH, DEC_SEQ * KV_PARTS, LANES)
        return jnp.pad(r, ((0, 0), (0, (PAGE_SIZE - DEC_SEQ) * KV_PARTS), (0, 0)))

    dq_s = dq[M_PROMPT:].reshape(DEC_BATCH, DEC_SEQ * DIFF_HEADS, 2, DIFF_HD)
    zq = jnp.zeros((DEC_BATCH, DEC_SEQ * DIFF_HEADS, DIFF_HD), BF16)
    qa_diff = jnp.concatenate([jnp.concatenate([dq_s[:, :, 0], zq], axis=-1),
                               jnp.concatenate([zq, dq_s[:, :, 1]], axis=-1)], axis=1)
    o_diff_s = decode_attention("diff", page_table, cache_diff, l, qa_diff, new_page(new_diff),
                                lam=diff_lambda, subln=diff_subln, lam_init=lam_init)

    fq_s = fq[M_PROMPT:].reshape(DEC_BATCH, DEC_SEQ, FOX_HEADS, FOX_HD)
    first = (jnp.arange(FOX_HEADS) < FOX_GROUP)[None, None, :, None]
    zf = jnp.zeros_like(fq_s)
    qa_fox = jnp.concatenate([jnp.where(first, fq_s, zf), jnp.where(first, zf, fq_s)], axis=-1)
    qa_fox = qa_fox.reshape(DEC_BATCH, DEC_ROWS, 2 * FOX_HD)
    nl_t = jnp.transpose(new_logf[M_PROMPT:].reshape(DEC_BATCH, DEC_SEQ, FOX_HEADS), (0, 2, 1))
    nl_t = jnp.pad(nl_t, ((0, 0), (0, 0), (0, LANES - DEC_SEQ)))
    d_past, cn = fox_cum_decode(page_table, cache_logf_t, l, nl_t)
    o_fox_s = decode_attention("fox", page_table, cache_fox, l, qa_fox, new_page(new_fox), cn=cn, d=d_past)

    qc_s = qcat[M_PROMPT:].reshape(DEC_BATCH, DEC_ROWS, MLA_QW)
    mla_new = jnp.transpose(new_mla[M_PROMPT:].reshape(DEC_BATCH, DEC_SEQ, MLA_KV_LORA + MLA_ROPE), (0, 2, 1))
    mla_new = jnp.pad(mla_new, ((0, 0), (0, 0), (0, PAGE_SIZE - DEC_SEQ)))
    o_lat_s = decode_attention("mla", page_table, cache_mla, l, qc_s[..., :MLA_KV_LORA], mla_new,
                               qb=qc_s[..., MLA_KV_LORA:MLA_KV_LORA + MLA_ROPE])

    o_diff = jnp.concatenate([o_diff_p, o_diff_s.reshape(M_SAMPLE, BRANCH_W)], axis=0)
    o_fox = jnp.concatenate([o_fox_p, o_fox_s.reshape(M_SAMPLE, BRANCH_W)], axis=0)
    o_lat = jnp.concatenate([o_lat_p, o_lat_s.reshape(M_SAMPLE, MLA_HEADS * MLA_KV_LORA)], axis=0)
    o_mla = mla_out_proj(o_lat, w_uv, tm=512)
    o_cat = jnp.concatenate([o_diff, o_fox, o_mla], axis=1)

    merged = gated_merge(o_cat, proj, b_gate, w_branch.astype(BF16), tm=TM_WIDE, tn=512)
    x = matmul_norm_res(merged, w_o.astype(BF16), g_mix_post, x, tm=512, tk=2048, name="out_proj")
    h = norm_matmul(x, 0, D_MODEL, g_mlp_pre, w_up.astype(BF16), tm=TM_WIDE, tn=1024, act="relu2",
                    out_dtype=BF16, name="mlp_up")
    x = matmul_norm_res(h, w_down.astype(BF16), g_mlp_post, x, tm=512, tk=2048, name="mlp_down")
    return x, (new_diff, new_fox, new_logf, new_mla)


def kernel(x_prompt, x_sample, cache_diff_kv, cache_fox_kv, cache_fox_logf, cache_mla_kv, page_table,
           g_mix_pre, w_in, b_fox_f, b_gate, diff_lambda, diff_subln, g_mla_qa, w_mla_qb, g_mla_kva,
           w_mla_kvb, w_branch, w_o, g_mix_post, g_mlp_pre, w_up, w_down, g_mlp_post):
    n_phys = cache_diff_kv.shape[1]
    caches = (cache_diff_kv.reshape(DEPTH, n_phys, PAGE_SIZE * KV_PARTS, LANES),
              cache_fox_kv.reshape(DEPTH, n_phys, PAGE_SIZE * KV_PARTS, LANES),
              jnp.transpose(cache_fox_logf, (0, 1, 3, 2)),
              jnp.transpose(cache_mla_kv, (0, 1, 3, 2)))
    tabs = _rope_tables()
    x = jnp.concatenate([x_prompt.reshape(M_PROMPT, D_MODEL), x_sample.reshape(M_SAMPLE, D_MODEL)], axis=0)
    states = []
    for l in range(DEPTH):
        lw = (g_mix_pre[l], w_in[l], b_fox_f[l], b_gate[l], diff_lambda[l], diff_subln[l], g_mla_qa[l],
              w_mla_qb[l], g_mla_kva[l], w_mla_kvb[l], w_branch[l], w_o[l], g_mix_post[l], g_mlp_pre[l],
              w_up[l], w_down[l], g_mlp_post[l])
        x, st = _layer(l, x, tabs, caches, page_table, lw)
        states.append(st)

    def out(i, shape_p, shape_s):
        rows = jnp.stack([s[i] for s in states], axis=0)
        return (rows[:, :M_PROMPT].reshape((DEPTH, BATCH, SEQ) + shape_p),
                rows[:, M_PROMPT:].reshape((DEPTH, DEC_BATCH, DEC_SEQ) + shape_s))

    diff_p, diff_s = out(0, (2, 1, 2 * DIFF_HD), (2, 1, 2 * DIFF_HD))
    fox_p, fox_s = out(1, (2, FOX_KV_HEADS, FOX_HD), (2, FOX_KV_HEADS, FOX_HD))
    logf_p, logf_s = out(2, (FOX_HEADS,), (FOX_HEADS,))
    mla_p, mla_s = out(3, (MLA_KV_LORA + MLA_ROPE,), (MLA_KV_LORA + MLA_ROPE,))
    y_p = x[:M_PROMPT].reshape(BATCH, SEQ, D_MODEL)
    y_s = x[M_PROMPT:].reshape(DEC_BATCH, DEC_SEQ, D_MODEL)
    return (y_p, y_s, diff_p, fox_p, logf_p, mla_p, diff_s, fox_s, logf_s, mla_s)
```

```python
import functools
import math

import numpy as np
import jax
import jax.numpy as jnp
from jax import lax
from jax.experimental import pallas as pl
from jax.experimental.pallas import tpu as pltpu

D_MODEL = 2048
BATCH = 4
SEQ = 2048
DEPTH = 2
DEC_BATCH = 128
DEC_SEQ = 4
PAST_LEN = 8192
PAGE_SIZE = 128
N_PAGES = PAST_LEN // PAGE_SIZE
BRANCH_W = D_MODEL // 2
DIFF_HD = 128
DIFF_HEADS = BRANCH_W // (2 * DIFF_HD)
FOX_HD = 128
FOX_HEADS = BRANCH_W // FOX_HD
FOX_KV_HEADS = 2
FOX_GROUP = FOX_HEADS // FOX_KV_HEADS
MLA_V = 128
MLA_HEADS = BRANCH_W // MLA_V
MLA_NOPE = 128
MLA_ROPE = 64
MLA_Q_LORA = D_MODEL // 4
MLA_KV_LORA = D_MODEL // 8
N_BRANCH = 3
D_FF = 4 * D_MODEL
ROPE_THETA = 10000.0
EPS = 1e-6

M_PROMPT = BATCH * SEQ
M_SAMPLE = DEC_BATCH * DEC_SEQ
M_ALL = M_PROMPT + M_SAMPLE
TM_WIDE = M_ALL // 8

LANES = 128
NEG = -1e30
VMEM_LIMIT = 48 * 1024 * 1024

C_DQ = 0
C_FQ = 1024
C_DKV = 2048
C_FKV = 2560
C_MQA = 3072
C_MKV = 3584
C_GATE = 4096
IN_COLS_PAD = C_GATE + N_BRANCH * D_MODEL
MLA_QW = MLA_KV_LORA + LANES
MLA_KW = MLA_KV_LORA + LANES

BF16 = jnp.bfloat16
F32 = jnp.float32


def _cparams(sem):
    return pltpu.CompilerParams(dimension_semantics=sem, vmem_limit_bytes=VMEM_LIMIT)


def _dot(a, b):
    return jnp.dot(a, b, preferred_element_type=F32)


def _dot_nt(a, b):
    return lax.dot_general(a, b, (((1,), (1,)), ((), ())), preferred_element_type=F32)


def _dot_exact(a, b):
    return jnp.dot(a, b, preferred_element_type=F32, precision=lax.Precision.HIGHEST)


def _rms(x, g):
    ms = jnp.mean(x * x, axis=-1, keepdims=True)
    return x * lax.rsqrt(ms + EPS) * g


def _norm_matmul_kernel(x_ref, g_ref, w_ref, o_ref, u_ref, *, act):
    @pl.when(pl.program_id(1) == 0)
    def _():
        u_ref[...] = _rms(x_ref[...].astype(F32), g_ref[...]).astype(BF16)

    acc = _dot(u_ref[...], w_ref[...])
    if act == "relu2":
        acc = jnp.square(jnp.maximum(acc, 0.0))
    o_ref[...] = acc.astype(o_ref.dtype)


def norm_matmul(x, x_col, k, g, w, *, tm, tn, act=None, out_dtype=F32, name):
    m = x.shape[0]
    n = w.shape[1]
    return pl.pallas_call(
        functools.partial(_norm_matmul_kernel, act=act),
        out_shape=jax.ShapeDtypeStruct((m, n), out_dtype),
        grid=(m // tm, n // tn),
        in_specs=[
            pl.BlockSpec((tm, k), lambda i, j: (i, x_col)),
            pl.BlockSpec((1, k), lambda i, j: (0, 0)),
            pl.BlockSpec((k, tn), lambda i, j: (0, j)),
        ],
        out_specs=pl.BlockSpec((tm, tn), lambda i, j: (i, j)),
        scratch_shapes=[pltpu.VMEM((tm, k), BF16)],
        compiler_params=_cparams(("parallel", "arbitrary")),
        name=name,
    )(x, g.reshape(1, k), w)


def _matmul_norm_res_kernel(a_ref, w_ref, g_ref, r_ref, o_ref, acc_ref, *, nk):
    k = pl.program_id(1)

    @pl.when(k == 0)
    def _():
        acc_ref[...] = jnp.zeros_like(acc_ref)

    acc_ref[...] += _dot(a_ref[...], w_ref[...])

    @pl.when(k == nk - 1)
    def _():
        o_ref[...] = r_ref[...] + _rms(acc_ref[...], g_ref[...])


def matmul_norm_res(a, w, g, resid, *, tm, tk, name):
    m, kdim = a.shape
    n = w.shape[1]
    nk = kdim // tk
    return pl.pallas_call(
        functools.partial(_matmul_norm_res_kernel, nk=nk),
        out_shape=jax.ShapeDtypeStruct((m, n), F32),
        grid=(m // tm, nk),
        in_specs=[
            pl.BlockSpec((tm, tk), lambda i, k: (i, k)),
            pl.BlockSpec((tk, n), lambda i, k: (k, 0)),
            pl.BlockSpec((1, n), lambda i, k: (0, 0)),
            pl.BlockSpec((tm, n), lambda i, k: (i, 0)),
        ],
        out_specs=pl.BlockSpec((tm, n), lambda i, k: (i, 0)),
        scratch_shapes=[pltpu.VMEM((tm, n), F32)],
        compiler_params=_cparams(("parallel", "arbitrary")),
        name=name,
    )(a, w, g.reshape(1, n), resid)


def _merge_kernel(o0_ref, o1_ref, o2_ref, g0_ref, g1_ref, g2_ref, b_ref, wb_ref, m_ref):
    acc = None
    for n, (o_ref, g_ref) in enumerate(((o0_ref, g0_ref), (o1_ref, g1_ref), (o2_ref, g2_ref))):
        y = _dot(o_ref[...], wb_ref[n])
        term = jax.nn.sigmoid(g_ref[...] + b_ref[n]) * y
        acc = term if acc is None else acc + term
    m_ref[...] = acc.astype(m_ref.dtype)


def gated_merge(o_branches, proj, b_gate, w_branch, *, tm, tn):
    m = proj.shape[0]
    nj = D_MODEL // tn
    gate_spec = lambda n: pl.BlockSpec(
        (tm, tn), lambda i, j, n=n: (i, (C_GATE + n * D_MODEL) // tn + j))
    return pl.pallas_call(
        _merge_kernel,
        out_shape=jax.ShapeDtypeStruct((m, D_MODEL), BF16),
        grid=(m // tm, nj),
        in_specs=[
            pl.BlockSpec((tm, BRANCH_W), lambda i, j: (i, 0)),
            pl.BlockSpec((tm, BRANCH_W), lambda i, j: (i, 0)),
            pl.BlockSpec((tm, BRANCH_W), lambda i, j: (i, 0)),
            gate_spec(0), gate_spec(1), gate_spec(2),
            pl.BlockSpec((N_BRANCH, 1, tn), lambda i, j: (0, 0, j)),
            pl.BlockSpec((N_BRANCH, BRANCH_W, tn), lambda i, j: (0, 0, j)),
        ],
        out_specs=pl.BlockSpec((tm, tn), lambda i, j: (i, j)),
        compiler_params=_cparams(("parallel", "arbitrary")),
        name="gated_merge",
    )(*o_branches, proj, proj, proj, b_gate.reshape(N_BRANCH, 1, D_MODEL), w_branch)


def _rope128(x, cos_f, sin_s):
    return x * cos_f + pltpu.roll(x, DIFF_HD // 2, 1) * sin_s


def _rope64(x, cos_f, sin_s):
    lane = lax.broadcasted_iota(jnp.int32, x.shape, 1)
    half = MLA_ROPE // 2
    rot = jnp.where((lane % MLA_ROPE) < half,
                    pltpu.roll(x, LANES - half, 1), pltpu.roll(x, half, 1))
    return x * cos_f + rot * sin_s


def _log_sigmoid(x):
    return jnp.minimum(x, 0.0) - jnp.log1p(jnp.exp(-jnp.abs(x)))


def _inproj_epilogue_kernel(dq_ref, fq_ref, dkv_ref, fkv_ref, mkv_ref, c128_ref, s128_ref,
                            c64_ref, s64_ref, gkv_ref, bf_ref,
                            dq_o, fq_o, dkv_o, dkv_bf_o, fkv_bf_o, logf_o, logfp_o, mla_o, mla_bf_o):
    cos_f, sin_s = c128_ref[...], s128_ref[...]
    qscale = DIFF_HD ** -0.5
    for c in range(2 * DIFF_HEADS):
        sl = slice(c * DIFF_HD, (c + 1) * DIFF_HD)
        dq_o[:, sl] = (_rope128(dq_ref[:, sl], cos_f, sin_s) * qscale).astype(BF16)
    fq_o[...] = (fq_ref[...] * (FOX_HD ** -0.5)).astype(BF16)
    for c in range(2):
        sl = slice(c * DIFF_HD, (c + 1) * DIFF_HD)
        kr = _rope128(dkv_ref[:, sl], cos_f, sin_s)
        dkv_o[:, sl] = kr
        dkv_bf_o[:, sl] = kr.astype(BF16)
    v = dkv_ref[:, 2 * DIFF_HD:]
    dkv_o[:, 2 * DIFF_HD:] = v
    dkv_bf_o[:, 2 * DIFF_HD:] = v.astype(BF16)
    fkv_bf_o[...] = fkv_ref[...].astype(BF16)
    lf = _log_sigmoid(mkv_ref[:, MLA_KV_LORA + LANES:] + bf_ref[...])
    logfp_o[...] = lf
    logf_o[...] = lf[:, :FOX_HEADS]
    ckv = _rms(mkv_ref[:, :MLA_KV_LORA], gkv_ref[...])
    kpe = _rope64(mkv_ref[:, MLA_KV_LORA:MLA_KV_LORA + LANES], c64_ref[...], s64_ref[...])
    mla_o[:, :MLA_KV_LORA] = ckv
    mla_o[:, MLA_KV_LORA:] = kpe[:, :MLA_ROPE]
    mla_bf_o[:, :MLA_KV_LORA] = ckv.astype(BF16)
    mla_bf_o[:, MLA_KV_LORA:] = kpe.astype(BF16)


def inproj_epilogue(proj, tabs, g_kva, b_fox_pad, *, tm):
    m = proj.shape[0]
    c128, s128, c64, s64 = tabs
    row = lambda w, c: pl.BlockSpec((tm, w), lambda i, c=c: (i, c))
    full = lambda w: pl.BlockSpec((1, w), lambda i: (0, 0))
    outs = [
        ((m, 1024), BF16), ((m, 1024), BF16), ((m, 512), F32), ((m, 512), BF16), ((m, 512), BF16),
        ((m, FOX_HEADS), F32), ((m, LANES), F32), ((m, MLA_KV_LORA + MLA_ROPE), F32), ((m, MLA_KW), BF16),
    ]
    return pl.pallas_call(
        _inproj_epilogue_kernel,
        out_shape=[jax.ShapeDtypeStruct(s, d) for s, d in outs],
        grid=(m // tm,),
        in_specs=[
            row(1024, C_DQ // 1024), row(1024, C_FQ // 1024), row(512, C_DKV // 512),
            row(512, C_FKV // 512), row(512, C_MKV // 512),
            row(LANES, 0), row(LANES, 0), row(LANES, 0), row(LANES, 0),
            full(MLA_KV_LORA), full(LANES),
        ],
        out_specs=[pl.BlockSpec((tm, s[1]), lambda i: (i, 0)) for s, _ in outs],
        compiler_params=_cparams(("parallel",)),
        name="inproj_epilogue",
    )(proj, proj, proj, proj, proj, c128, s128, c64, s64, g_kva.reshape(1, -1), b_fox_pad)


def _mla_q_kernel(q_ref, wuk_ref, c64_ref, s64_ref, o_ref):
    scale = (MLA_NOPE + MLA_ROPE) ** -0.5
    for h in range(MLA_HEADS):
        qn = q_ref[:, h * MLA_NOPE:(h + 1) * MLA_NOPE].astype(BF16)
        qlat = _dot(qn, wuk_ref[h])
        base = MLA_HEADS * MLA_NOPE + h * LANES
        qpe = _rope64(q_ref[:, base:base + LANES], c64_ref[...], s64_ref[...])
        o_ref[:, h * MLA_QW:h * MLA_QW + MLA_KV_LORA] = (qlat * scale).astype(BF16)
        o_ref[:, h * MLA_QW + MLA_KV_LORA:(h + 1) * MLA_QW] = (qpe * scale).astype(BF16)


def mla_q_prep(q, w_uk, c64, s64, *, tm):
    m, qw = q.shape
    return pl.pallas_call(
        _mla_q_kernel,
        out_shape=jax.ShapeDtypeStruct((m, MLA_HEADS * MLA_QW), BF16),
        grid=(m // tm,),
        in_specs=[
            pl.BlockSpec((tm, qw), lambda i: (i, 0)),
            pl.BlockSpec((MLA_HEADS, MLA_NOPE, MLA_KV_LORA), lambda i: (0, 0, 0)),
            pl.BlockSpec((tm, LANES), lambda i: (i, 0)),
            pl.BlockSpec((tm, LANES), lambda i: (i, 0)),
        ],
        out_specs=pl.BlockSpec((tm, MLA_HEADS * MLA_QW), lambda i: (i, 0)),
        compiler_params=_cparams(("parallel",)),
        name="mla_q_prep",
    )(q, w_uk, c64, s64)


def _mla_out_kernel(o_ref, wuv_ref, y_ref):
    for h in range(MLA_HEADS):
        y = _dot(o_ref[:, h * MLA_KV_LORA:(h + 1) * MLA_KV_LORA], wuv_ref[h])
        y_ref[:, h * MLA_V:(h + 1) * MLA_V] = y.astype(BF16)


def mla_out_proj(o_lat, w_uv, *, tm):
    m = o_lat.shape[0]
    return pl.pallas_call(
        _mla_out_kernel,
        out_shape=jax.ShapeDtypeStruct((m, BRANCH_W), BF16),
        grid=(m // tm,),
        in_specs=[
            pl.BlockSpec((tm, MLA_HEADS * MLA_KV_LORA), lambda i: (i, 0)),
            pl.BlockSpec((MLA_HEADS, MLA_KV_LORA, MLA_V), lambda i: (0, 0, 0)),
        ],
        out_specs=pl.BlockSpec((tm, BRANCH_W), lambda i: (i, 0)),
        compiler_params=_cparams(("parallel",)),
        name="mla_out_proj",
    )(o_lat, w_uv)


CUM_BLK = 256


def _fox_cum_prompt_kernel(x_ref, xt_ref, cum_ref, cumt_ref):
    r = lax.broadcasted_iota(jnp.int32, (CUM_BLK, CUM_BLK), 0)
    c = lax.broadcasted_iota(jnp.int32, (CUM_BLK, CUM_BLK), 1)
    lower = (c <= r).astype(F32)
    upper = (r <= c).astype(F32)
    carry = jnp.zeros((1, LANES), F32)
    carry_t = jnp.zeros((FOX_HEADS, 1), F32)
    for b in range(SEQ // CUM_BLK):
        sl = slice(b * CUM_BLK, (b + 1) * CUM_BLK)
        cb = _dot_exact(lower, x_ref[sl, :]) + carry
        cum_ref[sl, :] = cb
        carry = cb[CUM_BLK - 1:CUM_BLK, :]
        cbt = _dot_exact(xt_ref[0, :, sl], upper) + carry_t
        cumt_ref[0, :, sl] = cbt
        carry_t = cbt[:, CUM_BLK - 1:CUM_BLK]


def fox_cum_prompt(logf_pad, logf_t):
    return pl.pallas_call(
        _fox_cum_prompt_kernel,
        out_shape=[jax.ShapeDtypeStruct((M_PROMPT, LANES), F32),
                   jax.ShapeDtypeStruct((BATCH, FOX_HEADS, SEQ), F32)],
        grid=(BATCH,),
        in_specs=[pl.BlockSpec((SEQ, LANES), lambda b: (b, 0)),
                  pl.BlockSpec((1, FOX_HEADS, SEQ), lambda b: (b, 0, 0))],
        out_specs=[pl.BlockSpec((SEQ, LANES), lambda b: (b, 0)),
                   pl.BlockSpec((1, FOX_HEADS, SEQ), lambda b: (b, 0, 0))],
        compiler_params=_cparams(("parallel",)),
        name="fox_cum_prompt",
    )(logf_pad, logf_t)


def _softmax_init(m_ref, l_ref, acc_ref):
    m_ref[...] = jnp.full_like(m_ref, NEG)
    l_ref[...] = jnp.zeros_like(l_ref)
    acc_ref[...] = jnp.zeros_like(acc_ref)


def _softmax_step(s, pv_fn, m_ref, l_ref, acc_ref, idx):
    m_prev = m_ref[idx]
    m_new = jnp.maximum(m_prev, jnp.max(s, axis=-1, keepdims=True))
    alpha = jnp.exp(m_prev - m_new)
    p = jnp.exp(s - m_new)
    l_ref[idx] = alpha * l_ref[idx] + jnp.sum(p, axis=-1, keepdims=True)
    acc_ref[idx] = alpha * acc_ref[idx] + pv_fn(p.astype(BF16))
    m_ref[idx] = m_new


def _causal_mask(qi, ki, tq, tk, nstack):
    qpos = qi * tq + lax.broadcasted_iota(jnp.int32, (nstack * tq, tk), 0) % tq
    kpos = ki * tk + lax.broadcasted_iota(jnp.int32, (nstack * tq, tk), 1)
    return kpos <= qpos


def _diff_lambda(lam_ref, lam_init):
    lf = lam_ref[...]
    a = jnp.sum(lf[0:1] * lf[1:2], axis=-1, keepdims=True)
    b = jnp.sum(lf[2:3] * lf[3:4], axis=-1, keepdims=True)
    return jnp.exp(a) - jnp.exp(b) + lam_init


def _diff_finish(o1, o2, lam, subln, lam_init):
    return _rms(o1 - lam * o2, subln) * (1.0 - lam_init)


def _flash_diff_kernel(q_ref, kv_ref, lam_ref, subln_ref, o_ref, m_ref, l_ref, acc_ref,
                       *, tq, tk, nk, lam_init):
    qi, ki = pl.program_id(1), pl.program_id(2)

    @pl.when(ki == 0)
    def _():
        _softmax_init(m_ref, l_ref, acc_ref)

    @pl.when(ki * tk <= qi * tq + tq - 1)
    def _():
        mask = _causal_mask(qi, ki, tq, tk, 1)
        v = kv_ref[:, 2 * DIFF_HD:]
        for c in range(2 * DIFF_HEADS):
            mp = c % 2
            s = _dot_nt(q_ref[:, c * DIFF_HD:(c + 1) * DIFF_HD], kv_ref[:, mp * DIFF_HD:(mp + 1) * DIFF_HD])
            s = jnp.where(mask, s, NEG)
            _softmax_step(s, lambda p: _dot(p, v), m_ref, l_ref, acc_ref, c)

    @pl.when(ki == nk - 1)
    def _():
        lam = _diff_lambda(lam_ref, lam_init)
        for h in range(DIFF_HEADS):
            o1 = acc_ref[2 * h] / l_ref[2 * h]
            o2 = acc_ref[2 * h + 1] / l_ref[2 * h + 1]
            o = _diff_finish(o1, o2, lam, subln_ref[...], lam_init)
            o_ref[:, h * 2 * DIFF_HD:(h + 1) * 2 * DIFF_HD] = o.astype(BF16)


def _kv_clamp(qi, ki, tq, tk):
    return jnp.minimum(ki, (qi * tq + tq - 1) // tk)


def flash_diff(dq, dkv_bf, diff_lambda, subln, lam_init, *, tq, tk):
    nq, nk = SEQ // tq, SEQ // tk
    return pl.pallas_call(
        functools.partial(_flash_diff_kernel, tq=tq, tk=tk, nk=nk, lam_init=lam_init),
        out_shape=jax.ShapeDtypeStruct((M_PROMPT, BRANCH_W), BF16),
        grid=(BATCH, nq, nk),
        in_specs=[
            pl.BlockSpec((tq, BRANCH_W), lambda b, qi, ki: (b * nq + qi, 0)),
            pl.BlockSpec((tk, 512), lambda b, qi, ki: (b * nk + _kv_clamp(qi, ki, tq, tk), 0)),
            pl.BlockSpec((4, DIFF_HD), lambda b, qi, ki: (0, 0)),
            pl.BlockSpec((1, 2 * DIFF_HD), lambda b, qi, ki: (0, 0)),
        ],
        out_specs=pl.BlockSpec((tq, BRANCH_W), lambda b, qi, ki: (b * nq + qi, 0)),
        scratch_shapes=[pltpu.VMEM((2 * DIFF_HEADS, tq, 1), F32), pltpu.VMEM((2 * DIFF_HEADS, tq, 1), F32),
                        pltpu.VMEM((2 * DIFF_HEADS, tq, 2 * DIFF_HD), F32)],
        compiler_params=_cparams(("parallel", "parallel", "arbitrary")),
        name="flash_diff",
    )(dq, dkv_bf, diff_lambda, subln.reshape(1, -1))


def _flash_fox_kernel(q_ref, k_ref, v_ref, cq_ref, ck_ref, o_ref, m_ref, l_ref, acc_ref, *, tq, tk, nk):
    g, qi, ki = pl.program_id(1), pl.program_id(2), pl.program_id(3)

    @pl.when(ki == 0)
    def _():
        _softmax_init(m_ref, l_ref, acc_ref)

    @pl.when(ki * tk <= qi * tq + tq - 1)
    def _():
        mask = _causal_mask(qi, ki, tq, tk, FOX_GROUP)
        k, v = k_ref[...], v_ref[...]
        lane = lax.broadcasted_iota(jnp.int32, (tq, LANES), 1)
        sub = lax.broadcasted_iota(jnp.int32, (FOX_HEADS, tk), 0)
        bias = []
        for hl in range(FOX_GROUP):
            h = g * FOX_GROUP + hl
            cq = jnp.sum(jnp.where(lane == h, cq_ref[...], 0.0), axis=-1, keepdims=True)
            ck = jnp.sum(jnp.where(sub == h, ck_ref[0], 0.0), axis=0, keepdims=True)
            bias.append(cq - ck)
        q = jnp.concatenate([q_ref[:, hl * FOX_HD:(hl + 1) * FOX_HD] for hl in range(FOX_GROUP)], axis=0)
        s = _dot_nt(q, k) + jnp.concatenate(bias, axis=0)
        s = jnp.where(mask, s, NEG)
        _softmax_step(s, lambda p: _dot(p, v), m_ref, l_ref, acc_ref, 0)

    @pl.when(ki == nk - 1)
    def _():
        for hl in range(FOX_GROUP):
            rows = slice(hl * tq, (hl + 1) * tq)
            o_ref[:, hl * FOX_HD:(hl + 1) * FOX_HD] = (acc_ref[0, rows] / l_ref[0, rows]).astype(BF16)


def flash_fox(fq, fkv_bf, cum_pad, cum_t, *, tq, tk):
    nq, nk = SEQ // tq, SEQ // tk
    gw = FOX_GROUP * FOX_HD
    return pl.pallas_call(
        functools.partial(_flash_fox_kernel, tq=tq, tk=tk, nk=nk),
        out_shape=jax.ShapeDtypeStruct((M_PROMPT, BRANCH_W), BF16),
        grid=(BATCH, FOX_KV_HEADS, nq, nk),
        in_specs=[
            pl.BlockSpec((tq, gw), lambda b, g, qi, ki: (b * nq + qi, g)),
            pl.BlockSpec((tk, FOX_HD), lambda b, g, qi, ki: (b * nk + _kv_clamp(qi, ki, tq, tk), g)),
            pl.BlockSpec((tk, FOX_HD), lambda b, g, qi, ki: (b * nk + _kv_clamp(qi, ki, tq, tk), FOX_KV_HEADS + g)),
            pl.BlockSpec((tq, LANES), lambda b, g, qi, ki: (b * nq + qi, 0)),
            pl.BlockSpec((1, FOX_HEADS, tk), lambda b, g, qi, ki: (b, 0, _kv_clamp(qi, ki, tq, tk))),
        ],
        out_specs=pl.BlockSpec((tq, gw), lambda b, g, qi, ki: (b * nq + qi, g)),
        scratch_shapes=[pltpu.VMEM((1, FOX_GROUP * tq, 1), F32), pltpu.VMEM((1, FOX_GROUP * tq, 1), F32),
                        pltpu.VMEM((1, FOX_GROUP * tq, FOX_HD), F32)],
        compiler_params=_cparams(("parallel", "parallel", "parallel", "arbitrary")),
        name="flash_fox",
    )(fq, fkv_bf, fkv_bf, cum_pad, cum_t)


def _flash_mla_kernel(q_ref, kv_ref, o_ref, m_ref, l_ref, acc_ref, *, tq, tk, nk):
    qi, ki = pl.program_id(1), pl.program_id(2)

    @pl.when(ki == 0)
    def _():
        _softmax_init(m_ref, l_ref, acc_ref)

    @pl.when(ki * tk <= qi * tq + tq - 1)
    def _():
        mask = _causal_mask(qi, ki, tq, tk, 1)
        kv = kv_ref[...]
        v = kv_ref[:, :MLA_KV_LORA]
        for h in range(MLA_HEADS):
            s = _dot_nt(q_ref[:, h * MLA_QW:(h + 1) * MLA_QW], kv)
            s = jnp.where(mask, s, NEG)
            _softmax_step(s, lambda p: _dot(p, v), m_ref, l_ref, acc_ref, h)

    @pl.when(ki == nk - 1)
    def _():
        for h in range(MLA_HEADS):
            o_ref[:, h * MLA_KV_LORA:(h + 1) * MLA_KV_LORA] = (acc_ref[h] / l_ref[h]).astype(BF16)


def flash_mla(qcat, mla_bf, *, tq, tk):
    nq, nk = SEQ // tq, SEQ // tk
    return pl.pallas_call(
        functools.partial(_flash_mla_kernel, tq=tq, tk=tk, nk=nk),
        out_shape=jax.ShapeDtypeStruct((M_PROMPT, MLA_HEADS * MLA_KV_LORA), BF16),
        grid=(BATCH, nq, nk),
        in_specs=[
            pl.BlockSpec((tq, MLA_HEADS * MLA_QW), lambda b, qi, ki: (b * nq + qi, 0)),
            pl.BlockSpec((tk, MLA_KW), lambda b, qi, ki: (b * nk + _kv_clamp(qi, ki, tq, tk), 0)),
        ],
        out_specs=pl.BlockSpec((tq, MLA_HEADS * MLA_KV_LORA), lambda b, qi, ki: (b * nq + qi, 0)),
        scratch_shapes=[pltpu.VMEM((MLA_HEADS, tq, 1), F32), pltpu.VMEM((MLA_HEADS, tq, 1), F32),
                        pltpu.VMEM((MLA_HEADS, tq, MLA_KV_LORA), F32)],
        compiler_params=_cparams(("parallel", "parallel", "arbitrary")),
        name="flash_mla",
    )(qcat, mla_bf)


def _split3(x):
    hi = x.astype(BF16)
    r = x - hi.astype(F32)
    mid = r.astype(BF16)
    lo = (r - mid.astype(F32)).astype(BF16)
    return hi, mid, lo


def _fox_cum_decode_kernel(pt_ref, nl_ref, *rest):
    page_refs = rest[:N_PAGES]
    d_ref, cn_ref = rest[N_PAGES:]
    rows = N_PAGES * FOX_HEADS
    j = lax.broadcasted_iota(jnp.int32, (LANES, LANES), 0)
    s = lax.broadcasted_iota(jnp.int32, (LANES, LANES), 1)
    incl = (j <= s).astype(F32).astype(BF16)
    strict = (j > s).astype(F32).astype(BF16)
    hi, mid, lo = _split3(nl_ref[0])
    cn_ref[0] = _dot(hi, incl) + _dot(mid, incl) + _dot(lo, incl)

    x = jnp.concatenate([p[0, 0] for p in page_refs], axis=0)
    hi, mid, lo = _split3(x)
    d_in = _dot(hi, strict) + _dot(mid, strict) + _dot(lo, strict)
    tot = jnp.broadcast_to(jnp.sum(x, axis=-1, keepdims=True), (rows, LANES))
    later = jnp.zeros((FOX_HEADS, LANES), F32)
    for p in range(N_PAGES - 1, -1, -1):
        sl = slice(p * FOX_HEADS, (p + 1) * FOX_HEADS)
        d_ref[0, p] = d_in[sl] + later
        later = later + tot[sl]


def fox_cum_decode(page_table, logf_t_cache, layer, new_logf_t):
    page_spec = lambda p: pl.BlockSpec(
        (1, 1, FOX_HEADS, LANES), lambda b, pt, p=p: (layer, pt[b, p], 0, 0))
    grid_spec = pltpu.PrefetchScalarGridSpec(
        num_scalar_prefetch=1,
        grid=(DEC_BATCH,),
        in_specs=[pl.BlockSpec((1, FOX_HEADS, LANES), lambda b, pt: (b, 0, 0))]
        + [page_spec(p) for p in range(N_PAGES)],
        out_specs=[pl.BlockSpec((1, N_PAGES, FOX_HEADS, LANES), lambda b, pt: (b, 0, 0, 0)),
                   pl.BlockSpec((1, FOX_HEADS, LANES), lambda b, pt: (b, 0, 0))],
    )
    return pl.pallas_call(
        _fox_cum_decode_kernel,
        out_shape=[jax.ShapeDtypeStruct((DEC_BATCH, N_PAGES, FOX_HEADS, LANES), F32),
                   jax.ShapeDtypeStruct((DEC_BATCH, FOX_HEADS, LANES), F32)],
        grid_spec=grid_spec,
        compiler_params=_cparams(("parallel",)),
        name="fox_cum_decode",
    )(page_table, new_logf_t, *([logf_t_cache] * N_PAGES))


DEC_ROWS = 32
DEC_KEYS = (N_PAGES + 1) * PAGE_SIZE
KV_PARTS = 4


def _decode_kernel(pt_ref, *refs, variant, lam_init):
    it = iter(refs)
    qa_ref = next(it)
    qb_ref = next(it) if variant == "mla" else None
    new_ref = next(it)
    if variant == "fox":
        cn_ref, d_ref = next(it), next(it)
    if variant == "diff":
        lam_ref, subln_ref = next(it), next(it)
    page_refs = [next(it) for _ in range(N_PAGES)]
    o_ref, s_ref, p_ref = next(it), next(it), next(it)
    qa = qa_ref[0]

    def rd(p, idx):
        return page_refs[p][(0, 0) + idx] if p < N_PAGES else new_ref[(0,) + idx]

    def part(p, j):
        return rd(p, (pl.ds(j, PAGE_SIZE, stride=KV_PARTS), slice(None)))

    def scores(p):
        if variant == "mla":
            kt = rd(p, (slice(0, MLA_KV_LORA), slice(None))).astype(BF16)
            pet = rd(p, (slice(MLA_KV_LORA, MLA_KV_LORA + MLA_ROPE), slice(None))).astype(BF16)
            return _dot(qa, kt) + _dot(qb_ref[0], pet)
        k = jnp.concatenate([part(p, 0), part(p, 1)], axis=1).astype(BF16)
        return _dot_nt(qa, k)

    def weighted_values(pm, p):
        if variant == "mla":
            return _dot_nt(pm, rd(p, (slice(0, MLA_KV_LORA), slice(None))).astype(BF16))
        v = jnp.concatenate([part(p, 2), part(p, 3)], axis=1).astype(BF16)
        return _dot(pm, v)

    if variant == "fox":
        cn = cn_ref[0]
        cn_t = [jnp.broadcast_to(cn[:, t:t + 1], (FOX_HEADS, LANES)) for t in range(DEC_SEQ)]

    for p in range(N_PAGES + 1):
        s = scores(p)
        if variant == "fox":
            d_page = d_ref[0, p] if p < N_PAGES else -cn
            s = s + jnp.concatenate([d_page + c for c in cn_t], axis=0)
        if p == N_PAGES:
            row = lax.broadcasted_iota(jnp.int32, (DEC_ROWS, PAGE_SIZE), 0)
            key = lax.broadcasted_iota(jnp.int32, (DEC_ROWS, PAGE_SIZE), 1)
            tok = (row % (DEC_SEQ * DIFF_HEADS)) // DIFF_HEADS if variant == "diff" else row // FOX_HEADS
            s = jnp.where(key <= tok, s, NEG)
        s_ref[:, p * PAGE_SIZE:(p + 1) * PAGE_SIZE] = s

    s = s_ref[...]
    e = jnp.exp(s - jnp.max(s, axis=-1, keepdims=True))
    l = jnp.sum(e, axis=-1, keepdims=True)
    p_ref[...] = e.astype(BF16)

    acc = None
    for p in range(N_PAGES + 1):
        t = weighted_values(p_ref[:, p * PAGE_SIZE:(p + 1) * PAGE_SIZE], p)
        acc = t if acc is None else acc + t

    o = acc / l
    if variant == "diff":
        half = DEC_ROWS // 2
        lam = _diff_lambda(lam_ref, lam_init)
        o = _diff_finish(o[:half], o[half:], lam, subln_ref[...], lam_init)
    elif variant == "fox":
        row = lax.broadcasted_iota(jnp.int32, (DEC_ROWS, FOX_HD), 0)
        o = jnp.where(row % FOX_HEADS < FOX_GROUP, o[:, :FOX_HD], o[:, FOX_HD:])
    o_ref[0] = o.astype(BF16)


def decode_attention(variant, page_table, cache, layer, qa, new_page, *, qb=None, cn=None, d=None,
                     lam=None, subln=None, lam_init=0.0):
    page_shape = tuple(cache.shape[2:])
    per_seq = lambda shape: pl.BlockSpec((1,) + shape, lambda b, pt: (b,) + (0,) * len(shape))
    const = lambda shape: pl.BlockSpec(shape, lambda b, pt: (0,) * len(shape))
    in_specs = [per_seq((DEC_ROWS, 2 * LANES))]
    args = [qa]
    if variant == "mla":
        in_specs.append(per_seq((DEC_ROWS, MLA_ROPE)))
        args.append(qb)
    in_specs.append(per_seq(page_shape))
    args.append(new_page)
    if variant == "fox":
        in_specs += [per_seq((FOX_HEADS, LANES)), per_seq((N_PAGES, FOX_HEADS, LANES))]
        args += [cn, d]
    if variant == "diff":
        in_specs += [const((4, DIFF_HD)), const((1, 2 * DIFF_HD))]
        args += [lam, subln.reshape(1, -1)]
    for p in range(N_PAGES):
        in_specs.append(pl.BlockSpec((1, 1) + page_shape, lambda b, pt, p=p: (layer, pt[b, p], 0, 0)))
        args.append(cache)
    out_rows = DEC_ROWS // 2 if variant == "diff" else DEC_ROWS
    out_w = FOX_HD if variant == "fox" else 2 * LANES
    grid_spec = pltpu.PrefetchScalarGridSpec(
        num_scalar_prefetch=1,
        grid=(DEC_BATCH,),
        in_specs=in_specs,
        out_specs=pl.BlockSpec((1, out_rows, out_w), lambda b, pt: (b, 0, 0)),
        scratch_shapes=[pltpu.VMEM((DEC_ROWS, DEC_KEYS), F32), pltpu.VMEM((DEC_ROWS, DEC_KEYS), BF16)],
    )
    return pl.pallas_call(
        functools.partial(_decode_kernel, variant=variant, lam_init=lam_init),
        out_shape=jax.ShapeDtypeStruct((DEC_BATCH, out_rows, out_w), BF16),
        grid_spec=grid_spec,
        compiler_params=_cparams(("parallel",)),
        name="decode_" + variant,
    )(page_table, *args)


def _rope_tables():
    pos = jnp.concatenate([jnp.tile(jnp.arange(SEQ, dtype=jnp.int32), BATCH),
                           jnp.tile(PAST_LEN + jnp.arange(DEC_SEQ, dtype=jnp.int32), DEC_BATCH)])

    def tab(dim):
        half = dim // 2
        inv_freq = 1.0 / (ROPE_THETA ** (jnp.arange(half, dtype=F32) / half))
        ang = pos.astype(F32)[:, None] * inv_freq[None, :]
        cos, sin = jnp.cos(ang), jnp.sin(ang)
        reps = LANES // dim
        return (jnp.tile(jnp.concatenate([cos, cos], axis=1), (1, reps)),
                jnp.tile(jnp.concatenate([-sin, sin], axis=1), (1, reps)))

    return tab(DIFF_HD) + tab(MLA_ROPE)


def _prep_w_in(w_in):
    sizes = (1024, 256, 256, 1024, 256, 256, FOX_HEADS, MLA_Q_LORA, MLA_KV_LORA + MLA_ROPE, N_BRANCH * D_MODEL)
    offs = np.concatenate([[0], np.cumsum(sizes)])
    dq, dk, dv, fq, fk, fv, ff, mqa, mkva, gate = [w_in[:, offs[i]:offs[i + 1]] for i in range(len(sizes))]
    z = lambda n: jnp.zeros((D_MODEL, n), w_in.dtype)
    cols = [dq, fq, dk, dv, fk, fv, mqa, mkva[:, :MLA_KV_LORA], mkva[:, MLA_KV_LORA:], z(LANES - MLA_ROPE),
            ff, z(LANES - FOX_HEADS), gate]
    return jnp.concatenate(cols, axis=1).astype(BF16)


def _prep_w_qb(w_qb):
    nope = w_qb[:, :, :MLA_NOPE].reshape(MLA_Q_LORA, MLA_HEADS * MLA_NOPE)
    pe = jnp.pad(w_qb[:, :, MLA_NOPE:], ((0, 0), (0, 0), (0, LANES - MLA_ROPE)))
    return jnp.concatenate([nope, pe.reshape(MLA_Q_LORA, MLA_HEADS * LANES)], axis=1).astype(BF16)


def _layer(l, x, tabs, caches, page_table, lw):
    (g_mix_pre, w_in, b_fox_f, b_gate, diff_lambda, diff_subln, g_mla_qa, w_mla_qb, g_mla_kva,
     w_mla_kvb, w_branch, w_o, g_mix_post, g_mlp_pre, w_up, w_down, g_mlp_post) = lw
    cache_diff, cache_fox, cache_logf_t, cache_mla = caches
    c128, s128, c64, s64 = tabs
    lam_init = 0.8 - 0.6 * math.exp(-0.3 * l)

    proj = norm_matmul(x, 0, D_MODEL, g_mix_pre, _prep_w_in(w_in), tm=TM_WIDE, tn=1024, name="in_proj")
    b_fox_pad = jnp.pad(b_fox_f, (0, LANES - FOX_HEADS)).reshape(1, LANES)
    (dq, fq, new_diff, dkv_bf, fkv_bf, new_logf, logf_pad, new_mla, mla_bf) = inproj_epilogue(
        proj, tabs, g_mla_kva, b_fox_pad, tm=256)
    new_fox = proj[:, C_FKV:C_FKV + 512]

    q = norm_matmul(proj, C_MQA // MLA_Q_LORA, MLA_Q_LORA, g_mla_qa, _prep_w_qb(w_mla_qb),
                    tm=512, tn=512, name="mla_q_proj")
    w_uk = jnp.transpose(w_mla_kvb[:, :, :MLA_NOPE], (1, 2, 0)).astype(BF16)
    w_uv = jnp.transpose(w_mla_kvb[:, :, MLA_NOPE:], (1, 0, 2)).astype(BF16)
    qcat = mla_q_prep(q, w_uk, c64, s64, tm=256)

    logf_t = jnp.transpose(new_logf[:M_PROMPT].reshape(BATCH, SEQ, FOX_HEADS), (0, 2, 1))
    cum_pad, cum_t = fox_cum_prompt(logf_pad, logf_t)
    o_diff_p = flash_diff(dq, dkv_bf, diff_lambda, diff_subln, lam_init, tq=512, tk=1024)
    o_fox_p = flash_fox(fq, fkv_bf, cum_pad, cum_t, tq=256, tk=1024)
    o_lat_p = flash_mla(qcat, mla_bf, tq=256, tk=1024)

    def new_page(rows):
        r = rows[M_PROMPT:].reshape(DEC_BATCH, DEC_SEQ * KV_PARTS, LANES)
        return jnp.pad(r, ((0, 0), (0, (PAGE_SIZE - DEC_SEQ) * KV_PARTS), (0, 0)))

    dq_s = dq[M_PROMPT:].reshape(DEC_BATCH, DEC_SEQ * DIFF_HEADS, 2, DIFF_HD)
    zq = jnp.zeros((DEC_BATCH, DEC_SEQ * DIFF_HEADS, DIFF_HD), BF16)
    qa_diff = jnp.concatenate([jnp.concatenate([dq_s[:, :, 0], zq], axis=-1),
                               jnp.concatenate([zq, dq_s[:, :, 1]], axis=-1)], axis=1)
    o_diff_s = decode_attention("diff", page_table, cache_diff, l, qa_diff, new_page(new_diff),
                                lam=diff_lambda, subln=diff_subln, lam_init=lam_init)

    fq_s = fq[M_PROMPT:].reshape(DEC_BATCH, DEC_SEQ, FOX_HEADS, FOX_HD)
    first = (jnp.arange(FOX_HEADS) < FOX_GROUP)[None, None, :, None]
    zf = jnp.zeros_like(fq_s)
    qa_fox = jnp.concatenate([jnp.where(first, fq_s, zf), jnp.where(first, zf, fq_s)], axis=-1)
    qa_fox = qa_fox.reshape(DEC_BATCH, DEC_ROWS, 2 * FOX_HD)
    nl_t = jnp.transpose(new_logf[M_PROMPT:].reshape(DEC_BATCH, DEC_SEQ, FOX_HEADS), (0, 2, 1))
    nl_t = jnp.pad(nl_t, ((0, 0), (0, 0), (0, LANES - DEC_SEQ)))
    d_past, cn = fox_cum_decode(page_table, cache_logf_t, l, nl_t)
    o_fox_s = decode_attention("fox", page_table, cache_fox, l, qa_fox, new_page(new_fox), cn=cn, d=d_past)

    qc_s = qcat[M_PROMPT:].reshape(DEC_BATCH, DEC_ROWS, MLA_QW)
    mla_new = jnp.transpose(new_mla[M_PROMPT:].reshape(DEC_BATCH, DEC_SEQ, MLA_KV_LORA + MLA_ROPE), (0, 2, 1))
    mla_new = jnp.pad(mla_new, ((0, 0), (0, 0), (0, PAGE_SIZE - DEC_SEQ)))
    o_lat_s = decode_attention("mla", page_table, cache_mla, l, qc_s[..., :MLA_KV_LORA], mla_new,
                               qb=qc_s[..., MLA_KV_LORA:MLA_KV_LORA + MLA_ROPE])

    o_diff = jnp.concatenate([o_diff_p, o_diff_s.reshape(M_SAMPLE, BRANCH_W)], axis=0)
    o_fox = jnp.concatenate([o_fox_p, o_fox_s.reshape(M_SAMPLE, BRANCH_W)], axis=0)
    o_lat = jnp.concatenate([o_lat_p, o_lat_s.reshape(M_SAMPLE, MLA_HEADS * MLA_KV_LORA)], axis=0)
    o_mla = mla_out_proj(o_lat, w_uv, tm=512)

    merged = gated_merge((o_diff, o_fox, o_mla), proj, b_gate, w_branch.astype(BF16), tm=TM_WIDE, tn=512)
    x = matmul_norm_res(merged, w_o.astype(BF16), g_mix_post, x, tm=512, tk=2048, name="out_proj")
    h = norm_matmul(x, 0, D_MODEL, g_mlp_pre, w_up.astype(BF16), tm=TM_WIDE, tn=1024, act="relu2",
                    out_dtype=BF16, name="mlp_up")
    x = matmul_norm_res(h, w_down.astype(BF16), g_mlp_post, x, tm=512, tk=2048, name="mlp_down")
    return x, (new_diff, new_fox, new_logf, new_mla)


def kernel(x_prompt, x_sample, cache_diff_kv, cache_fox_kv, cache_fox_logf, cache_mla_kv, page_table,
           g_mix_pre, w_in, b_fox_f, b_gate, diff_lambda, diff_subln, g_mla_qa, w_mla_qb, g_mla_kva,
           w_mla_kvb, w_branch, w_o, g_mix_post, g_mlp_pre, w_up, w_down, g_mlp_post):
    n_phys = cache_diff_kv.shape[1]
    caches = (cache_diff_kv.reshape(DEPTH, n_phys, PAGE_SIZE * KV_PARTS, LANES),
              cache_fox_kv.reshape(DEPTH, n_phys, PAGE_SIZE * KV_PARTS, LANES),
              jnp.transpose(cache_fox_logf, (0, 1, 3, 2)),
              jnp.transpose(cache_mla_kv, (0, 1, 3, 2)))
    tabs = _rope_tables()
    x = jnp.concatenate([x_prompt.reshape(M_PROMPT, D_MODEL), x_sample.reshape(M_SAMPLE, D_MODEL)], axis=0)
    states = []
    for l in range(DEPTH):
        lw = (g_mix_pre[l], w_in[l], b_fox_f[l], b_gate[l], diff_lambda[l], diff_subln[l], g_mla_qa[l],
              w_mla_qb[l], g_mla_kva[l], w_mla_kvb[l], w_branch[l], w_o[l], g_mix_post[l], g_mlp_pre[l],
              w_up[l], w_down[l], g_mlp_post[l])
        x, st = _layer(l, x, tabs, caches, page_table, lw)
        states.append(st)

    def out(i, shape_p, shape_s):
        rows = jnp.stack([s[i] for s in states], axis=0)
        return (rows[:, :M_PROMPT].reshape((DEPTH, BATCH, SEQ) + shape_p),
                rows[:, M_PROMPT:].reshape((DEPTH, DEC_BATCH, DEC_SEQ) + shape_s))

    diff_p, diff_s = out(0, (2, 1, 2 * DIFF_HD), (2, 1, 2 * DIFF_HD))
    fox_p, fox_s = out(1, (2, FOX_KV_HEADS, FOX_HD), (2, FOX_KV_HEADS, FOX_HD))
    logf_p, logf_s = out(2, (FOX_HEADS,), (FOX_HEADS,))
    mla_p, mla_s = out(3, (MLA_KV_LORA + MLA_ROPE,), (MLA_KV_LORA + MLA_ROPE,))
    y_p = x[:M_PROMPT].reshape(BATCH, SEQ, D_MODEL)
    y_s = x[M_PROMPT:].reshape(DEC_BATCH, DEC_SEQ, D_MODEL)
    return (y_p, y_s, diff_p, fox_p, logf_p, mla_p, diff_s, fox_s, logf_s, mla_s)
```
